```python
import math
import jax, jax.numpy as jnp
from jax import lax
import numpy as np

D_MODEL = 1024
BATCH = 2
SEQ = 16384
DEPTH = 1

CHUNK = 64
Q_BLOCK = 128
D_MIX = D_MODEL
A_HEADS = 4
A_QK_DIM = 64
A_V_DIM = 2 * A_QK_DIM
A_WIDTH = A_HEADS * A_V_DIM
A_QK_COLS = A_HEADS * 2 * A_QK_DIM
B_HEADS = 8
B_HEAD_DIM = 64
B_WIDTH = B_HEADS * B_HEAD_DIM
N_BUCKETS = 32
MAX_DISTANCE = 128
DEEPNORM_ALPHA = (2 * DEPTH) ** 0.25
DEEPNORM_BETA = (8 * DEPTH) ** -0.25
LN_EPS = 1e-5
NEG_INF = -1e30
N_IN = 2 * A_QK_COLS + 2 * A_WIDTH + 4 * B_WIDTH + B_HEADS

kernel_name = "hybrid_diffattn_fox_adaln_deepnorm"


def _layernorm(x):
    xf = x.astype(jnp.float32)
    mu = jnp.mean(xf, -1, keepdims=True)
    var = jnp.mean(jnp.square(xf - mu), -1, keepdims=True)
    return ((xf - mu) * lax.rsqrt(var + LN_EPS)).astype(x.dtype)


def _layernorm_affine(x, g, b):
    xf = x.astype(jnp.float32)
    mu = jnp.mean(xf, -1, keepdims=True)
    var = jnp.mean(jnp.square(xf - mu), -1, keepdims=True)
    y = (xf - mu) * lax.rsqrt(var + LN_EPS) * g.astype(jnp.float32) + b.astype(jnp.float32)
    return y.astype(x.dtype)


def _t5_bucket(rel):
    nb = N_BUCKETS // 2
    ret = jnp.where(rel > 0, nb, 0)
    n = jnp.abs(rel)
    max_exact = nb // 2
    nf = jnp.maximum(n, 1).astype(jnp.float32)
    large = max_exact + (jnp.log(nf / max_exact) / math.log(MAX_DISTANCE / max_exact) * (nb - max_exact)).astype(jnp.int32)
    large = jnp.minimum(large, nb - 1)
    return ret + jnp.where(n < max_exact, n, large)


def _diff_attention(q, k, v, rel_bias, lam, lam_init, norm_g):
    bsz, seq = q.shape[0], q.shape[1]
    q = jnp.transpose(q, (0, 2, 3, 1, 4)) * (A_QK_DIM ** -0.5)
    k = jnp.transpose(k, (0, 2, 3, 1, 4))
    v = jnp.transpose(v, (0, 2, 1, 3))
    pos = jnp.arange(seq, dtype=jnp.int32)
    outs = []
    for q0 in range(0, seq, Q_BLOCK):
        q1 = q0 + Q_BLOCK
        qp = pos[q0:q1]
        kp = pos[:q1]
        logits = jnp.einsum('bhmqd,bhmkd->bhmqk', q[:, :, :, q0:q1], k[:, :, :, :q1]).astype(jnp.float32)
        bias = jnp.transpose(rel_bias[_t5_bucket(kp[None, :] - qp[:, None])], (2, 0, 1)).astype(jnp.float32)
        mask = (kp[None, :] // CHUNK) <= (qp[:, None] // CHUNK)
        logits = jnp.where(mask, logits + bias[None, :, None], NEG_INF)
        p = jax.nn.softmax(logits, axis=-1)
        attn = p[:, :, 0] - lam * p[:, :, 1]
        outs.append(jnp.einsum('bhqk,bhkd->bhqd', attn.astype(v.dtype), v[:, :, :q1]))
    o = jnp.concatenate(outs, axis=2).astype(jnp.float32)
    o = o * lax.rsqrt(jnp.mean(jnp.square(o), -1, keepdims=True) + LN_EPS)
    o = o * norm_g.astype(jnp.float32) * (1.0 - lam_init)
    return jnp.transpose(o, (0, 2, 1, 3)).reshape(bsz, seq, A_WIDTH).astype(v.dtype)


def _forgetting_attention(q, k, v, log_f):
    bsz, seq = q.shape[0], q.shape[1]
    q = jnp.transpose(q, (0, 2, 1, 3)) * (B_HEAD_DIM ** -0.5)
    k = jnp.transpose(k, (0, 2, 1, 3))
    v = jnp.transpose(v, (0, 2, 1, 3))
    cf = jnp.transpose(jnp.cumsum(log_f, axis=1), (0, 2, 1))
    pos = jnp.arange(seq, dtype=jnp.int32)
    outs = []
    for q0 in range(0, seq, Q_BLOCK):
        q1 = q0 + Q_BLOCK
        qp = pos[q0:q1]
        kp = pos[:q1]
        logits = jnp.einsum('bhqd,bhkd->bhqk', q[:, :, q0:q1], k[:, :, :q1]).astype(jnp.float32)
        logits = logits + cf[:, :, q0:q1, None] - cf[:, :, None, :q1]
        mask = kp[None, :] <= qp[:, None]
        p = jax.nn.softmax(jnp.where(mask, logits, NEG_INF), axis=-1)
        outs.append(jnp.einsum('bhqk,bhkd->bhqd', p.astype(v.dtype), v[:, :, :q1]))
    o = jnp.concatenate(outs, axis=2)
    return jnp.transpose(o, (0, 2, 1, 3)).reshape(bsz, seq, B_WIDTH)


def setup_inputs(seed: int = 0) -> dict:
    key = jax.random.key(seed)
    ks = jax.random.split(key, 16)
    f32 = jnp.float32
    x = jax.random.normal(ks[0], (BATCH, SEQ, D_MODEL), f32)
    c = jax.random.normal(ks[1], (BATCH, D_MODEL), f32)
    col_scale = np.ones((N_IN,), np.float32)
    a_v0 = 2 * A_QK_COLS
    col_scale[a_v0:a_v0 + A_WIDTH] = DEEPNORM_BETA
    b_v0 = 2 * A_QK_COLS + 2 * A_WIDTH + 2 * B_WIDTH
    col_scale[b_v0:b_v0 + B_WIDTH] = DEEPNORM_BETA
    w_in = jax.random.normal(ks[2], (DEPTH, D_MODEL, N_IN), f32) * (D_MODEL ** -0.5) * jnp.asarray(col_scale)
    w_out = jax.random.normal(ks[3], (DEPTH, D_MIX, D_MODEL), f32) * (D_MIX ** -0.5) * DEEPNORM_BETA
    rel_bias = jax.random.normal(ks[4], (N_BUCKETS, A_HEADS), f32) * 0.5
    lam_q1 = jax.random.normal(ks[5], (DEPTH, A_QK_DIM), f32) * 0.1
    lam_k1 = jax.random.normal(ks[6], (DEPTH, A_QK_DIM), f32) * 0.1
    lam_q2 = jax.random.normal(ks[7], (DEPTH, A_QK_DIM), f32) * 0.1
    lam_k2 = jax.random.normal(ks[8], (DEPTH, A_QK_DIM), f32) * 0.1
    diff_norm_g = 1.0 + 0.02 * jax.random.normal(ks[9], (DEPTH, A_V_DIM), f32)
    b_forget = 3.0 + 0.5 * jax.random.normal(ks[10], (DEPTH, B_HEADS), f32)
    w_ada = jax.random.normal(ks[11], (DEPTH, D_MODEL, 3 * D_MODEL), f32) * (D_MODEL ** -0.5) * 0.5
    b_ada = 0.02 * jax.random.normal(ks[12], (DEPTH, 3 * D_MODEL), f32)
    ln_g = 1.0 + 0.02 * jax.random.normal(ks[13], (DEPTH, D_MODEL), f32)
    ln_b = 0.02 * jax.random.normal(ks[14], (DEPTH, D_MODEL), f32)
    return {"x": x, "c": c, "w_in": w_in, "w_out": w_out, "rel_bias": rel_bias,
            "lam_q1": lam_q1, "lam_k1": lam_k1, "lam_q2": lam_q2, "lam_k2": lam_k2,
            "diff_norm_g": diff_norm_g, "b_forget": b_forget, "w_ada": w_ada, "b_ada": b_ada,
            "ln_g": ln_g, "ln_b": ln_b}


def reference(x, c, w_in, w_out, rel_bias, lam_q1, lam_k1, lam_q2, lam_k2,
              diff_norm_g, b_forget, w_ada, b_ada, ln_g, ln_b):
    bsz, seq = x.shape[0], x.shape[1]
    sizes = [A_QK_COLS, A_QK_COLS, A_WIDTH, A_WIDTH, B_WIDTH, B_WIDTH, B_WIDTH, B_WIDTH, B_HEADS]
    offsets = np.cumsum(sizes)[:-1].tolist()
    for layer in range(DEPTH):
        ada = jax.nn.silu(c) @ w_ada[layer] + b_ada[layer]
        shift, scale, gate = jnp.split(ada, 3, axis=-1)
        h = _layernorm(x) * (1.0 + scale[:, None]) + shift[:, None]
        proj = h @ w_in[layer]
        aq, ak, av, ag, bq, bk, bv, bg, bf = jnp.split(proj, offsets, axis=-1)
        lam_init = 0.8 - 0.6 * math.exp(-0.3 * layer)
        lam = (jnp.exp(jnp.sum(lam_q1[layer].astype(jnp.float32) * lam_k1[layer].astype(jnp.float32)))
               - jnp.exp(jnp.sum(lam_q2[layer].astype(jnp.float32) * lam_k2[layer].astype(jnp.float32)))
               + lam_init)
        ya = _diff_attention(aq.reshape(bsz, seq, A_HEADS, 2, A_QK_DIM),
                             ak.reshape(bsz, seq, A_HEADS, 2, A_QK_DIM),
                             av.reshape(bsz, seq, A_HEADS, A_V_DIM),
                             rel_bias, lam, lam_init, diff_norm_g[layer])
        log_f = jax.nn.log_sigmoid(bf.astype(jnp.float32) + b_forget[layer].astype(jnp.float32))
        yb = _forgetting_attention(bq.reshape(bsz, seq, B_HEADS, B_HEAD_DIM),
                                   bk.reshape(bsz, seq, B_HEADS, B_HEAD_DIM),
                                   bv.reshape(bsz, seq, B_HEADS, B_HEAD_DIM), log_f)
        y = jnp.concatenate([ya * jax.nn.silu(ag), yb * jax.nn.silu(bg)], axis=-1) @ w_out[layer]
        x = _layernorm_affine(DEEPNORM_ALPHA * x + gate[:, None] * y, ln_g[layer], ln_b[layer])
    return x
```

```python
import functools
import math

import jax
import jax.numpy as jnp
from jax import lax
from jax.experimental import pallas as pl
from jax.experimental.pallas import tpu as pltpu

F32 = jnp.float32
BF16 = jnp.bfloat16

LANES = 128
HALF = 64
A_HEADS = 4
B_HEADS = 8
PAIRS = 4
CHUNK = 64
N_BUCKETS = 32
MAX_DISTANCE = 128
LN_EPS = 1e-5
NEG_INF = -1e30
LOG2E = 1.4426950408889634
DEPTH = 1
DEEPNORM_ALPHA = (2 * DEPTH) ** 0.25
LAM_INIT = 0.8 - 0.6 * math.exp(-0.3 * 0)

T = 512
TM = 512
VMEM_LIMIT = 60 * 1024 * 1024


def _ada_kernel(c_ref, w_ref, b_ref, q1_ref, k1_ref, q2_ref, k2_ref, ada_ref, lam_ref):
    c = c_ref[...]
    sc = c * jax.nn.sigmoid(c)
    ada_ref[...] = jnp.dot(sc, w_ref[...], precision=lax.Precision.HIGHEST,
                           preferred_element_type=F32) + b_ref[...]
    s1 = jnp.sum(q1_ref[...] * k1_ref[...], axis=-1, keepdims=True)
    s2 = jnp.sum(q2_ref[...] * k2_ref[...], axis=-1, keepdims=True)
    lam = jnp.exp(s1) - jnp.exp(s2) + LAM_INIT
    lam_ref[...] = jnp.broadcast_to(lam, lam_ref.shape)


def _ada_call(c8, w_ada, b_ada, q1, k1, q2, k2):
    d3 = w_ada.shape[1]
    return pl.pallas_call(
        _ada_kernel,
        out_shape=(jax.ShapeDtypeStruct((c8.shape[0], d3), F32),
                   jax.ShapeDtypeStruct((8, LANES), F32)),
        compiler_params=pltpu.CompilerParams(vmem_limit_bytes=VMEM_LIMIT),
        name="ada",
    )(c8, w_ada, b_ada, q1, k1, q2, k2)


def _t5_bucket(rel):
    nb = N_BUCKETS // 2
    ret = jnp.where(rel > 0, nb, 0)
    n = jnp.abs(rel)
    max_exact = nb // 2
    nf = jnp.maximum(n, 1).astype(F32)
    large = max_exact + (jnp.log(nf / max_exact) / math.log(MAX_DISTANCE / max_exact)
                         * (nb - max_exact)).astype(jnp.int32)
    large = jnp.minimum(large, nb - 1)
    return ret + jnp.where(n < max_exact, n, large)


def _table_kernel(rb_ref, tab_ref):
    i = pl.program_id(0)
    h = jnp.minimum(i, A_HEADS - 1)
    kl = lax.broadcasted_iota(jnp.int32, (T, T), 0)
    ql = lax.broadcasted_iota(jnp.int32, (T, T), 1)
    far = rb_ref[N_BUCKETS // 2 - 1, h]
    for d in range(2):
        rel = kl - ql + (d - 1) * T
        bucket = _t5_bucket(rel)
        bias = jnp.zeros((T, T), F32)
        for bk in range(N_BUCKETS):
            bias = jnp.where(bucket == bk, rb_ref[bk, h], bias)
        tab_a = (bias - far) * LOG2E
        if d == 1:
            tab_a = jnp.where((kl // CHUNK) <= (ql // CHUNK), tab_a, NEG_INF)
            tab_b = jnp.where(kl <= ql, 0.0, NEG_INF)
        else:
            tab_b = jnp.zeros((T, T), F32)
        tab = jnp.where(i < A_HEADS, tab_a, tab_b)
        tab_ref[0, d, :, :T] = tab
        tab_ref[0, d, :, T:] = tab


def _table_call(rel_bias):
    return pl.pallas_call(
        _table_kernel,
        grid=(A_HEADS + 1,),
        in_specs=[pl.BlockSpec(memory_space=pltpu.SMEM)],
        out_specs=pl.BlockSpec((1, 2, T, 2 * T), lambda i: (i, 0, 0, 0)),
        out_shape=jax.ShapeDtypeStruct((A_HEADS + 1, 2, T, 2 * T), F32),
        compiler_params=pltpu.CompilerParams(
            dimension_semantics=("arbitrary",), vmem_limit_bytes=VMEM_LIMIT),
        name="tables",
    )(rel_bias)


def _split3(v):
    hi = v.astype(BF16)
    r1 = v - hi.astype(F32)
    mid = r1.astype(BF16)
    lo = (r1 - mid.astype(F32)).astype(BF16)
    return hi, mid, lo


def _proj_kernel(x_ref, sh_ref, sc_ref, wrow_ref, wt_ref, bfb_ref,
                 k_ref, g_ref, qt_ref, vt_ref, e_ref, carry_ref):
    t = pl.program_id(1)
    d = x_ref.shape[2]
    x = x_ref[0]
    mu = jnp.mean(x, axis=-1, keepdims=True)
    xc = x - mu
    var = jnp.mean(xc * xc, axis=-1, keepdims=True)
    h = (xc * lax.rsqrt(var + LN_EPS)) * (1.0 + sc_ref[0]) + sh_ref[0]
    hb = h.astype(BF16)

    r = jnp.dot(hb, wrow_ref[...], preferred_element_type=F32)
    k_ref[0] = r[:, :d].astype(BF16)
    g = r[:, d:2 * d]
    g_ref[0] = (g * jax.nn.sigmoid(g)).astype(BF16)

    z = r[:, 2 * d:] + bfb_ref[...]
    lane = lax.broadcasted_iota(jnp.int32, (TM, LANES), 1)
    logf = jnp.minimum(z, 0.0) - jnp.log1p(jnp.exp(-jnp.abs(z)))
    logf = jnp.where(lane < B_HEADS, logf, 0.0)
    ri = lax.broadcasted_iota(jnp.int32, (TM, TM), 0)
    ci = lax.broadcasted_iota(jnp.int32, (TM, TM), 1)
    tri = jnp.where(ci <= ri, 1.0, 0.0).astype(BF16)
    hi, mid, lo = _split3(logf)
    cs = jnp.dot(tri, jnp.concatenate([hi, mid, lo], axis=1), preferred_element_type=F32)

    @pl.when(t == 0)
    def _():
        carry_ref[...] = jnp.zeros_like(carry_ref)

    cf = (cs[:, :LANES] + cs[:, LANES:2 * LANES]) + cs[:, 2 * LANES:] + carry_ref[...]
    carry_ref[...] = cf[TM - 1:TM, :]
    vh, vm, vl = _split3(-cf * LOG2E)
    e = jnp.where(lane < B_HEADS, vh.astype(F32),
                  jnp.where(lane < 2 * B_HEADS, pltpu.roll(vm.astype(F32), B_HEADS, 1),
                            jnp.where(lane < 3 * B_HEADS,
                                      pltpu.roll(vl.astype(F32), 2 * B_HEADS, 1), 0.0)))
    e_ref[0] = e.astype(BF16)

    tt = lax.dot_general(wt_ref[...], hb, (((1,), (1,)), ((), ())),
                         preferred_element_type=F32)
    qt_ref[0] = (tt[:d] * (LOG2E * HALF ** -0.5)).astype(BF16)
    vt_ref[0, :, 0] = tt[d:].reshape(d // LANES, LANES, TM).astype(BF16)


def _proj_call(x, shift, scale, wrow, wt, bfb):
    b, s, d = x.shape
    nt = s // TM
    nrow = wrow.shape[1]
    return pl.pallas_call(
        _proj_kernel,
        grid=(b, nt),
        in_specs=[
            pl.BlockSpec((1, TM, d), lambda i, t: (i, t, 0)),
            pl.BlockSpec((1, 1, d), lambda i, t: (i, 0, 0)),
            pl.BlockSpec((1, 1, d), lambda i, t: (i, 0, 0)),
            pl.BlockSpec((d, nrow), lambda i, t: (0, 0)),
            pl.BlockSpec((2 * d, d), lambda i, t: (0, 0)),
            pl.BlockSpec((1, LANES), lambda i, t: (0, 0)),
        ],
        out_specs=(
            pl.BlockSpec((1, TM, d), lambda i, t: (i, t, 0)),
            pl.BlockSpec((1, TM, d), lambda i, t: (i, t, 0)),
            pl.BlockSpec((1, d, TM), lambda i, t: (i, 0, t)),
            pl.BlockSpec((1, d // LANES, 1, LANES, TM), lambda i, t: (i, 0, t, 0, 0)),
            pl.BlockSpec((1, TM, LANES), lambda i, t: (i, t, 0)),
        ),
        out_shape=(
            jax.ShapeDtypeStruct((b, s, d), BF16),
            jax.ShapeDtypeStruct((b, s, d), BF16),
            jax.ShapeDtypeStruct((b, d, s), BF16),
            jax.ShapeDtypeStruct((b, d // LANES, nt, LANES, TM), BF16),
            jax.ShapeDtypeStruct((b, s, LANES), BF16),
        ),
        scratch_shapes=[pltpu.VMEM((1, LANES), F32)],
        compiler_params=pltpu.CompilerParams(
            dimension_semantics=("arbitrary", "arbitrary"), vmem_limit_bytes=VMEM_LIMIT),
        name="proj",
    )(x, shift, scale, wrow, wt, bfb)


def _attn_kernel(*refs, is_a):
    if is_a:
        (qt_ref, k_ref, vt_ref, g_ref, tab_ref, lam_ref, ng_ref,
         y_ref, qtb_ref, m_ref, l_ref, acc_ref) = refs
        e_ref = None
    else:
        (qt_ref, k_ref, e_ref, vt_ref, g_ref, tab_ref,
         y_ref, qtb_ref, m_ref, l_ref, acc_ref) = refs
    p = pl.program_id(1)
    qi = pl.program_id(2)
    sub = T // TM

    qt = qt_ref[0].astype(F32)
    row = lax.broadcasted_iota(jnp.int32, (LANES, T), 0)
    q_lo = jnp.where(row < HALF, qt, 0.0)
    q_hi = jnp.where(row >= HALF, qt, 0.0)
    qtb_ref[:LANES, :] = jnp.concatenate([q_lo, q_hi], axis=1).astype(BF16)
    if not is_a:
        r2 = lax.broadcasted_iota(jnp.int32, (LANES, 2 * T), 0)
        c2 = lax.broadcasted_iota(jnp.int32, (LANES, 2 * T), 1)
        head = 2 * p + jnp.where(c2 >= T, 1, 0)
        sel = (r2 == head) | (r2 == head + B_HEADS) | (r2 == head + 2 * B_HEADS)
        qtb_ref[LANES:, :] = jnp.where(sel, 1.0, 0.0).astype(BF16)

    m_ref[...] = jnp.full_like(m_ref, NEG_INF)
    l_ref[...] = jnp.zeros_like(l_ref)
    acc_ref[...] = jnp.zeros_like(acc_ref)

    def tile(j, tab):
        rows = pl.ds(pl.multiple_of(j * T, T), T)
        kt = k_ref[0, rows, :]
        if not is_a:
            kt = jnp.concatenate([kt, e_ref[0, rows, :]], axis=1)
        s = jnp.dot(kt, qtb_ref[...], preferred_element_type=F32)
        if tab is not None:
            s = s + tab
        m_old = m_ref[...]
        m_new = jnp.maximum(m_old, jnp.max(s, axis=0, keepdims=True))
        alpha = jnp.exp2(m_old - m_new)
        pe = jnp.exp2(s - m_new)
        l_ref[...] = alpha * l_ref[...] + jnp.sum(pe, axis=0, keepdims=True)
        pb = pe.astype(BF16)
        pv = None
        for u in range(sub):
            part = jnp.dot(vt_ref[0, 0, j * sub + u], pb[u * TM:(u + 1) * TM],
                           preferred_element_type=F32)
            pv = part if pv is None else pv + part
        acc_ref[...] = alpha * acc_ref[...] + pv
        m_ref[...] = m_new

    n_far = jnp.maximum(qi - 1, 0) if is_a else qi

    def far_body(j, carry):
        tile(j, None)
        return carry

    lax.fori_loop(0, n_far, far_body, 0)
    if is_a:
        @pl.when(qi >= 1)
        def _():
            tile(qi - 1, tab_ref[0, 0])
    tile(qi, tab_ref[0, 1])

    o = acc_ref[...] * (1.0 / l_ref[...])
    if is_a:
        lam = lam_ref[0:1, 0:1]
        o2 = o[:, :T] - lam * o[:, T:]
        ms = jnp.mean(o2 * o2, axis=0, keepdims=True)
        o2 = o2 * lax.rsqrt(ms + LN_EPS) * ng_ref[...] * (1.0 - LAM_INIT)
    else:
        o2 = jnp.where(row < HALF, o[:, :T], o[:, T:])
    y = o2.T * g_ref[0].astype(F32)
    y_ref[0] = y.astype(BF16)


def _attn_call(qt, k, e, vt5, g, tab, lam, ng, *, is_a):
    b, d, s = qt.shape
    nq = s // T
    off = 0 if is_a else PAIRS
    kd = LANES if is_a else 2 * LANES
    in_specs = [
        pl.BlockSpec((1, LANES, T), lambda i, p, q: (i, p + off, q)),
        pl.BlockSpec((1, s, LANES), lambda i, p, q: (i, 0, p + off)),
    ]
    args = [qt, k]
    if not is_a:
        in_specs.append(pl.BlockSpec((1, s, LANES), lambda i, p, q: (i, 0, 0)))
        args.append(e)
    in_specs += [
        pl.BlockSpec((1, 1, s // TM, LANES, TM), lambda i, p, q: (i, p + off, 0, 0, 0)),
        pl.BlockSpec((1, T, LANES), lambda i, p, q: (i, q, p + off)),
    ]
    args += [vt5, g]
    if is_a:
        in_specs.append(pl.BlockSpec((1, 2, T, 2 * T), lambda i, p, q: (p, 0, 0, 0)))
        in_specs.append(pl.BlockSpec((8, LANES), lambda i, p, q: (0, 0)))
        in_specs.append(pl.BlockSpec((LANES, 1), lambda i, p, q: (0, 0)))
        args += [tab, lam, ng]
    else:
        in_specs.append(pl.BlockSpec((1, 2, T, 2 * T), lambda i, p, q: (A_HEADS, 0, 0, 0)))
        args.append(tab)
    return pl.pallas_call(
        functools.partial(_attn_kernel, is_a=is_a),
        grid=(b, PAIRS, nq),
        in_specs=in_specs,
        out_specs=pl.BlockSpec((1, T, LANES), lambda i, p, q: (i, q, p)),
        out_shape=jax.ShapeDtypeStruct((b, s, PAIRS * LANES), BF16),
        scratch_shapes=[
            pltpu.VMEM((kd, 2 * T), BF16),
            pltpu.VMEM((1, 2 * T), F32),
            pltpu.VMEM((1, 2 * T), F32),
            pltpu.VMEM((LANES, 2 * T), F32),
        ],
        compiler_params=pltpu.CompilerParams(
            dimension_semantics=("arbitrary", "arbitrary", "arbitrary"),
            vmem_limit_bytes=VMEM_LIMIT),
        name="attn_a" if is_a else "attn_b",
    )(*args)


def _out_kernel(ya_ref, yb_ref, w1_ref, w2_ref, x_ref, gate_ref, lng_ref, lnb_ref, o_ref):
    y = (jnp.dot(ya_ref[0], w1_ref[...], preferred_element_type=F32)
         + jnp.dot(yb_ref[0], w2_ref[...], preferred_element_type=F32))
    z = DEEPNORM_ALPHA * x_ref[0] + gate_ref[0] * y
    mu = jnp.mean(z, axis=-1, keepdims=True)
    zc = z - mu
    var = jnp.mean(zc * zc, axis=-1, keepdims=True)
    o_ref[0] = zc * lax.rsqrt(var + LN_EPS) * lng_ref[...] + lnb_ref[...]


def _out_call(ya, yb, w1, w2, x, gate, lng, lnb):
    b, s, d = x.shape
    half = ya.shape[2]
    return pl.pallas_call(
        _out_kernel,
        grid=(b, s // TM),
        in_specs=[
            pl.BlockSpec((1, TM, half), lambda i, t: (i, t, 0)),
            pl.BlockSpec((1, TM, half), lambda i, t: (i, t, 0)),
            pl.BlockSpec((half, d), lambda i, t: (0, 0)),
            pl.BlockSpec((half, d), lambda i, t: (0, 0)),
            pl.BlockSpec((1, TM, d), lambda i, t: (i, t, 0)),
            pl.BlockSpec((1, 1, d), lambda i, t: (i, 0, 0)),
            pl.BlockSpec((1, d), lambda i, t: (0, 0)),
            pl.BlockSpec((1, d), lambda i, t: (0, 0)),
        ],
        out_specs=pl.BlockSpec((1, TM, d), lambda i, t: (i, t, 0)),
        out_shape=jax.ShapeDtypeStruct((b, s, d), x.dtype),
        compiler_params=pltpu.CompilerParams(
            dimension_semantics=("arbitrary", "arbitrary"), vmem_limit_bytes=VMEM_LIMIT),
        name="out",
    )(ya, yb, w1, w2, x, gate, lng, lnb)


def kernel(x, c, w_in, w_out, rel_bias, lam_q1, lam_k1, lam_q2, lam_k2, diff_norm_g,
           b_forget, w_ada, b_ada, ln_g, ln_b):
    b, s, d = x.shape
    assert s % T == 0 and T == TM and d == PAIRS * 2 * LANES
    layer = 0
    aw = PAIRS * LANES

    c8 = jnp.pad(c, ((0, 8 - b), (0, 0)))
    ada8, lam = _ada_call(c8, w_ada[layer], b_ada[layer][None], lam_q1[layer][None],
                          lam_k1[layer][None], lam_q2[layer][None], lam_k2[layer][None])
    ada = ada8[:b]
    shift = ada[:, None, :d]
    scale = ada[:, None, d:2 * d]
    gate = ada[:, None, 2 * d:]

    w = w_in[layer]
    grp = lambda i: w[:, i * aw:(i + 1) * aw]
    wbf = jnp.pad(w[:, 8 * aw:], ((0, 0), (0, LANES - B_HEADS)))
    wrow = jnp.concatenate([grp(1), grp(5), grp(3), grp(7), wbf], axis=1).astype(BF16)
    wt = jnp.concatenate([grp(0), grp(4), grp(2), grp(6)], axis=1).T.astype(BF16)
    bfb = jnp.pad(b_forget[layer], (0, LANES - B_HEADS))[None].astype(F32)

    k, g, qt, vt5, e = _proj_call(x, shift, scale, wrow, wt, bfb)
    tab = _table_call(rel_bias)
    ng = diff_norm_g[layer].reshape(LANES, 1)
    ya = _attn_call(qt, k, None, vt5, g, tab, lam, ng, is_a=True)
    yb = _attn_call(qt, k, e, vt5, g, tab, None, None, is_a=False)

    wo = w_out[layer].astype(BF16)
    return _out_call(ya, yb, wo[:aw], wo[aw:], x, gate, ln_g[layer][None], ln_b[layer][None])
```

```python
import functools
import math

import jax
import jax.numpy as jnp
from jax import lax
from jax.experimental import pallas as pl
from jax.experimental.pallas import tpu as pltpu

F32 = jnp.float32
BF16 = jnp.bfloat16

LANES = 128
HALF = 64
A_HEADS = 4
B_HEADS = 8
PAIRS = 4
CHUNK = 64
N_BUCKETS = 32
MAX_DISTANCE = 128
LN_EPS = 1e-5
NEG_INF = -1e30
LOG2E = 1.4426950408889634
DEPTH = 1
DEEPNORM_ALPHA = (2 * DEPTH) ** 0.25
LAM_INIT = 0.8 - 0.6 * math.exp(-0.3 * 0)

T = 512
TM = 512
ONES_ROWS = 16
V_ROWS = LANES + ONES_ROWS
VMEM_LIMIT = 60 * 1024 * 1024


def _ada_kernel(c_ref, w_ref, b_ref, q1_ref, k1_ref, q2_ref, k2_ref, ada_ref, lam_ref):
    c = c_ref[...]
    sc = c * jax.nn.sigmoid(c)
    ada_ref[...] = jnp.dot(sc, w_ref[...], precision=lax.Precision.HIGHEST,
                           preferred_element_type=F32) + b_ref[...]
    s1 = jnp.sum(q1_ref[...] * k1_ref[...], axis=-1, keepdims=True)
    s2 = jnp.sum(q2_ref[...] * k2_ref[...], axis=-1, keepdims=True)
    lam = jnp.exp(s1) - jnp.exp(s2) + LAM_INIT
    lam_ref[...] = jnp.broadcast_to(lam, lam_ref.shape)


def _ada_call(c8, w_ada, b_ada, q1, k1, q2, k2):
    d3 = w_ada.shape[1]
    return pl.pallas_call(
        _ada_kernel,
        out_shape=(jax.ShapeDtypeStruct((c8.shape[0], d3), F32),
                   jax.ShapeDtypeStruct((8, LANES), F32)),
        compiler_params=pltpu.CompilerParams(vmem_limit_bytes=VMEM_LIMIT),
        name="ada",
    )(c8, w_ada, b_ada, q1, k1, q2, k2)


def _t5_bucket(rel):
    nb = N_BUCKETS // 2
    ret = jnp.where(rel > 0, nb, 0)
    n = jnp.abs(rel)
    max_exact = nb // 2
    nf = jnp.maximum(n, 1).astype(F32)
    large = max_exact + (jnp.log(nf / max_exact) / math.log(MAX_DISTANCE / max_exact)
                         * (nb - max_exact)).astype(jnp.int32)
    large = jnp.minimum(large, nb - 1)
    return ret + jnp.where(n < max_exact, n, large)


def _table_kernel(rb_ref, tab_ref):
    i = pl.program_id(0)
    h = jnp.minimum(i, A_HEADS - 1)
    kl = lax.broadcasted_iota(jnp.int32, (T, T), 0)
    ql = lax.broadcasted_iota(jnp.int32, (T, T), 1)
    far = rb_ref[N_BUCKETS // 2 - 1, h]
    for d in range(2):
        rel = kl - ql + (d - 1) * T
        bucket = _t5_bucket(rel)
        bias = jnp.zeros((T, T), F32)
        for bk in range(N_BUCKETS):
            bias = jnp.where(bucket == bk, rb_ref[bk, h], bias)
        tab_a = (bias - far) * LOG2E
        if d == 1:
            tab_a = jnp.where((kl // CHUNK) <= (ql // CHUNK), tab_a, NEG_INF)
            tab_b = jnp.where(kl <= ql, 0.0, NEG_INF)
        else:
            tab_b = jnp.zeros((T, T), F32)
        tab = jnp.where(i < A_HEADS, tab_a, tab_b)
        tab_ref[0, d, :, :T] = tab
        tab_ref[0, d, :, T:] = tab


def _table_call(rel_bias):
    return pl.pallas_call(
        _table_kernel,
        grid=(A_HEADS + 1,),
        in_specs=[pl.BlockSpec(memory_space=pltpu.SMEM)],
        out_specs=pl.BlockSpec((1, 2, T, 2 * T), lambda i: (i, 0, 0, 0)),
        out_shape=jax.ShapeDtypeStruct((A_HEADS + 1, 2, T, 2 * T), F32),
        compiler_params=pltpu.CompilerParams(
            dimension_semantics=("arbitrary",), vmem_limit_bytes=VMEM_LIMIT),
        name="tables",
    )(rel_bias)


def _split3(v):
    hi = v.astype(BF16)
    r1 = v - hi.astype(F32)
    mid = r1.astype(BF16)
    lo = (r1 - mid.astype(F32)).astype(BF16)
    return hi, mid, lo


def _proj_kernel(x_ref, sh_ref, sc_ref, wrow_ref, wt_ref, bfb_ref,
                 k_ref, g_ref, qt_ref, vt_ref, e_ref, carry_ref):
    t = pl.program_id(1)
    d = x_ref.shape[2]
    x = x_ref[0]
    mu = jnp.mean(x, axis=-1, keepdims=True)
    xc = x - mu
    var = jnp.mean(xc * xc, axis=-1, keepdims=True)
    h = (xc * lax.rsqrt(var + LN_EPS)) * (1.0 + sc_ref[0]) + sh_ref[0]
    hb = h.astype(BF16)

    r = jnp.dot(hb, wrow_ref[...], preferred_element_type=F32)
    k_ref[0] = r[:, :d].astype(BF16)
    g = r[:, d:2 * d]
    g_ref[0] = (g * jax.nn.sigmoid(g)).astype(BF16)

    z = r[:, 2 * d:] + bfb_ref[...]
    lane = lax.broadcasted_iota(jnp.int32, (TM, LANES), 1)
    logf = jnp.minimum(z, 0.0) - jnp.log1p(jnp.exp(-jnp.abs(z)))
    logf = jnp.where(lane < B_HEADS, logf, 0.0)
    ri = lax.broadcasted_iota(jnp.int32, (TM, TM), 0)
    ci = lax.broadcasted_iota(jnp.int32, (TM, TM), 1)
    tri = jnp.where(ci <= ri, 1.0, 0.0).astype(BF16)
    hi, mid, lo = _split3(logf)
    cs = jnp.dot(tri, jnp.concatenate([hi, mid, lo], axis=1), preferred_element_type=F32)

    @pl.when(t == 0)
    def _():
        carry_ref[...] = jnp.zeros_like(carry_ref)

    cf = (cs[:, :LANES] + cs[:, LANES:2 * LANES]) + cs[:, 2 * LANES:] + carry_ref[...]
    carry_ref[...] = cf[TM - 1:TM, :]
    vh, vm, vl = _split3(-cf * LOG2E)
    e = jnp.where(lane < B_HEADS, vh.astype(F32),
                  jnp.where(lane < 2 * B_HEADS, pltpu.roll(vm.astype(F32), B_HEADS, 1),
                            jnp.where(lane < 3 * B_HEADS,
                                      pltpu.roll(vl.astype(F32), 2 * B_HEADS, 1), 0.0)))
    e_ref[0] = e.astype(BF16)

    tt = lax.dot_general(wt_ref[...], hb, (((1,), (1,)), ((), ())),
                         preferred_element_type=F32)
    qt_ref[0] = (tt[:d] * (LOG2E * HALF ** -0.5)).astype(BF16)
    vt_ref[0, :, 0, :LANES, :] = tt[d:].reshape(d // LANES, LANES, TM).astype(BF16)
    vt_ref[0, :, 0, LANES:, :] = jnp.ones((d // LANES, ONES_ROWS, TM), BF16)


def _proj_call(x, shift, scale, wrow, wt, bfb):
    b, s, d = x.shape
    nt = s // TM
    nrow = wrow.shape[1]
    return pl.pallas_call(
        _proj_kernel,
        grid=(b, nt),
        in_specs=[
            pl.BlockSpec((1, TM, d), lambda i, t: (i, t, 0)),
            pl.BlockSpec((1, 1, d), lambda i, t: (i, 0, 0)),
            pl.BlockSpec((1, 1, d), lambda i, t: (i, 0, 0)),
            pl.BlockSpec((d, nrow), lambda i, t: (0, 0)),
            pl.BlockSpec((2 * d, d), lambda i, t: (0, 0)),
            pl.BlockSpec((1, LANES), lambda i, t: (0, 0)),
        ],
        out_specs=(
            pl.BlockSpec((1, TM, d), lambda i, t: (i, t, 0)),
            pl.BlockSpec((1, TM, d), lambda i, t: (i, t, 0)),
            pl.BlockSpec((1, d, TM), lambda i, t: (i, 0, t)),
            pl.BlockSpec((1, d // LANES, 1, V_ROWS, TM), lambda i, t: (i, 0, t, 0, 0)),
            pl.BlockSpec((1, TM, LANES), lambda i, t: (i, t, 0)),
        ),
        out_shape=(
            jax.ShapeDtypeStruct((b, s, d), BF16),
            jax.ShapeDtypeStruct((b, s, d), BF16),
            jax.ShapeDtypeStruct((b, d, s), BF16),
            jax.ShapeDtypeStruct((b, d // LANES, nt, V_ROWS, TM), BF16),
            jax.ShapeDtypeStruct((b, s, LANES), BF16),
        ),
        scratch_shapes=[pltpu.VMEM((1, LANES), F32)],
        compiler_params=pltpu.CompilerParams(
            dimension_semantics=("arbitrary", "arbitrary"), vmem_limit_bytes=VMEM_LIMIT),
        name="proj",
    )(x, shift, scale, wrow, wt, bfb)


def _attn_kernel(*refs, is_a):
    if is_a:
        (qt_ref, k_ref, vt_ref, g_ref, tab_ref, lam_ref, ng_ref,
         y_ref, qtb_ref, m_ref, acc_ref, s_ref, mt_ref) = refs
        e_ref = None
    else:
        (qt_ref, k_ref, e_ref, vt_ref, g_ref, tab_ref,
         y_ref, qtb_ref, m_ref, acc_ref, s_ref, mt_ref) = refs
    p = pl.program_id(1)
    qi = pl.program_id(2)

    qt = qt_ref[0].astype(F32)
    row = lax.broadcasted_iota(jnp.int32, (LANES, T), 0)
    q_lo = jnp.where(row < HALF, qt, 0.0)
    q_hi = jnp.where(row >= HALF, qt, 0.0)
    qtb_ref[:LANES, :] = jnp.concatenate([q_lo, q_hi], axis=1).astype(BF16)
    if not is_a:
        r2 = lax.broadcasted_iota(jnp.int32, (LANES, 2 * T), 0)
        c2 = lax.broadcasted_iota(jnp.int32, (LANES, 2 * T), 1)
        head = 2 * p + jnp.where(c2 >= T, 1, 0)
        sel = (r2 == head) | (r2 == head + B_HEADS) | (r2 == head + 2 * B_HEADS)
        qtb_ref[LANES:, :] = jnp.where(sel, 1.0, 0.0).astype(BF16)

    m_ref[...] = jnp.full_like(m_ref, NEG_INF)
    acc_ref[...] = jnp.zeros_like(acc_ref)

    def scores(j):
        rows = pl.ds(pl.multiple_of(j * T, T), T)
        kt = k_ref[0, rows, :]
        if not is_a:
            kt = jnp.concatenate([kt, e_ref[0, rows, :]], axis=1)
        return jnp.dot(kt, qtb_ref[...], preferred_element_type=F32)

    def softmax_pv(j, s, mt):
        m_old = m_ref[...]
        m_new = jnp.maximum(m_old, mt)
        alpha = jnp.exp2(m_old - m_new)
        pb = jnp.exp2(s - m_new).astype(BF16)
        pv = jnp.dot(vt_ref[0, 0, j], pb, preferred_element_type=F32)
        acc_ref[...] = alpha * acc_ref[...] + pv
        m_ref[...] = m_new

    def tile(j, tab):
        s = scores(j)
        if tab is not None:
            s = s + tab
        softmax_pv(j, s, jnp.max(s, axis=0, keepdims=True))

    def scores_to(slot, j):
        s = scores(j)
        s_ref[slot] = s
        mt_ref[slot] = jnp.max(s, axis=0, keepdims=True)

    tile(qi, tab_ref[0, 1])
    if is_a:
        @pl.when(qi >= 1)
        def _():
            tile(qi - 1, tab_ref[0, 0])
    n_far = jnp.maximum(qi - 1, 0) if is_a else qi
    n_pairs = lax.shift_right_logical(n_far, 1)

    @pl.when((n_far & 1) == 1)
    def _():
        tile(n_far - 1, None)

    @pl.when(n_pairs > 0)
    def _():
        scores_to(0, 0)

    def pair_body(i, carry):
        scores_to(1, 2 * i + 1)
        softmax_pv(2 * i, s_ref[0], mt_ref[0])
        scores_to(0, 2 * i + 2)
        softmax_pv(2 * i + 1, s_ref[1], mt_ref[1])
        return carry

    lax.fori_loop(0, n_pairs, pair_body, 0)

    o = acc_ref[:LANES] * (1.0 / acc_ref[LANES:LANES + 1])
    if is_a:
        lam = lam_ref[0:1, 0:1]
        o2 = o[:, :T] - lam * o[:, T:]
        ms = jnp.mean(o2 * o2, axis=0, keepdims=True)
        o2 = o2 * lax.rsqrt(ms + LN_EPS) * ng_ref[...] * (1.0 - LAM_INIT)
    else:
        o2 = jnp.where(row < HALF, o[:, :T], o[:, T:])
    y = o2.T * g_ref[0].astype(F32)
    y_ref[0] = y.astype(BF16)


def _attn_call(qt, k, e, vt5, g, tab, lam, ng, *, is_a):
    b, d, s = qt.shape
    nq = s // T
    off = 0 if is_a else PAIRS
    kd = LANES if is_a else 2 * LANES
    in_specs = [
        pl.BlockSpec((1, LANES, T), lambda i, p, q: (i, p + off, q)),
        pl.BlockSpec((1, s, LANES), lambda i, p, q: (i, 0, p + off)),
    ]
    args = [qt, k]
    if not is_a:
        in_specs.append(pl.BlockSpec((1, s, LANES), lambda i, p, q: (i, 0, 0)))
        args.append(e)
    in_specs += [
        pl.BlockSpec((1, 1, s // TM, V_ROWS, TM), lambda i, p, q: (i, p + off, 0, 0, 0)),
        pl.BlockSpec((1, T, LANES), lambda i, p, q: (i, q, p + off)),
    ]
    args += [vt5, g]
    if is_a:
        in_specs.append(pl.BlockSpec((1, 2, T, 2 * T), lambda i, p, q: (p, 0, 0, 0)))
        in_specs.append(pl.BlockSpec((8, LANES), lambda i, p, q: (0, 0)))
        in_specs.append(pl.BlockSpec((LANES, 1), lambda i, p, q: (0, 0)))
        args += [tab, lam, ng]
    else:
        in_specs.append(pl.BlockSpec((1, 2, T, 2 * T), lambda i, p, q: (A_HEADS, 0, 0, 0)))
        args.append(tab)
    return pl.pallas_call(
        functools.partial(_attn_kernel, is_a=is_a),
        grid=(b, PAIRS, nq),
        in_specs=in_specs,
        out_specs=pl.BlockSpec((1, T, LANES), lambda i, p, q: (i, q, p)),
        out_shape=jax.ShapeDtypeStruct((b, s, PAIRS * LANES), BF16),
        scratch_shapes=[
            pltpu.VMEM((kd, 2 * T), BF16),
            pltpu.VMEM((1, 2 * T), F32),
            pltpu.VMEM((V_ROWS, 2 * T), F32),
            pltpu.VMEM((2, T, 2 * T), F32),
            pltpu.VMEM((2, 1, 2 * T), F32),
        ],
        compiler_params=pltpu.CompilerParams(
            dimension_semantics=("arbitrary", "arbitrary", "arbitrary"),
            vmem_limit_bytes=VMEM_LIMIT),
        name="attn_a" if is_a else "attn_b",
    )(*args)


def _out_kernel(ya_ref, yb_ref, w1_ref, w2_ref, x_ref, gate_ref, lng_ref, lnb_ref, o_ref):
    y = (jnp.dot(ya_ref[0], w1_ref[...], preferred_element_type=F32)
         + jnp.dot(yb_ref[0], w2_ref[...], preferred_element_type=F32))
    z = DEEPNORM_ALPHA * x_ref[0] + gate_ref[0] * y
    mu = jnp.mean(z, axis=-1, keepdims=True)
    zc = z - mu
    var = jnp.mean(zc * zc, axis=-1, keepdims=True)
    o_ref[0] = zc * lax.rsqrt(var + LN_EPS) * lng_ref[...] + lnb_ref[...]


def _out_call(ya, yb, w1, w2, x, gate, lng, lnb):
    b, s, d = x.shape
    half = ya.shape[2]
    return pl.pallas_call(
        _out_kernel,
        grid=(b, s // TM),
        in_specs=[
            pl.BlockSpec((1, TM, half), lambda i, t: (i, t, 0)),
            pl.BlockSpec((1, TM, half), lambda i, t: (i, t, 0)),
            pl.BlockSpec((half, d), lambda i, t: (0, 0)),
            pl.BlockSpec((half, d), lambda i, t: (0, 0)),
            pl.BlockSpec((1, TM, d), lambda i, t: (i, t, 0)),
            pl.BlockSpec((1, 1, d), lambda i, t: (i, 0, 0)),
            pl.BlockSpec((1, d), lambda i, t: (0, 0)),
            pl.BlockSpec((1, d), lambda i, t: (0, 0)),
        ],
        out_specs=pl.BlockSpec((1, TM, d), lambda i, t: (i, t, 0)),
        out_shape=jax.ShapeDtypeStruct((b, s, d), x.dtype),
        compiler_params=pltpu.CompilerParams(
            dimension_semantics=("arbitrary", "arbitrary"), vmem_limit_bytes=VMEM_LIMIT),
        name="out",
    )(ya, yb, w1, w2, x, gate, lng, lnb)


def kernel(x, c, w_in, w_out, rel_bias, lam_q1, lam_k1, lam_q2, lam_k2, diff_norm_g,
           b_forget, w_ada, b_ada, ln_g, ln_b):
    b, s, d = x.shape
    assert s % T == 0 and T == TM and d == PAIRS * 2 * LANES
    layer = 0
    aw = PAIRS * LANES

    c8 = jnp.pad(c, ((0, 8 - b), (0, 0)))
    ada8, lam = _ada_call(c8, w_ada[layer], b_ada[layer][None], lam_q1[layer][None],
                          lam_k1[layer][None], lam_q2[layer][None], lam_k2[layer][None])
    ada = ada8[:b]
    shift = ada[:, None, :d]
    scale = ada[:, None, d:2 * d]
    gate = ada[:, None, 2 * d:]

    w = w_in[layer]
    grp = lambda i: w[:, i * aw:(i + 1) * aw]
    wbf = jnp.pad(w[:, 8 * aw:], ((0, 0), (0, LANES - B_HEADS)))
    wrow = jnp.concatenate([grp(1), grp(5), grp(3), grp(7), wbf], axis=1).astype(BF16)
    wt = jnp.concatenate([grp(0), grp(4), grp(2), grp(6)], axis=1).T.astype(BF16)
    bfb = jnp.pad(b_forget[layer], (0, LANES - B_HEADS))[None].astype(F32)

    k, g, qt, vt5, e = _proj_call(x, shift, scale, wrow, wt, bfb)
    tab = _table_call(rel_bias)
    ng = diff_norm_g[layer].reshape(LANES, 1)
    ya = _attn_call(qt, k, None, vt5, g, tab, lam, ng, is_a=True)
    yb = _attn_call(qt, k, e, vt5, g, tab, None, None, is_a=False)

    wo = w_out[layer].astype(BF16)
    return _out_call(ya, yb, wo[:aw], wo[aw:], x, gate, ln_g[layer][None], ln_b[layer][None])
```

```python
import functools
import math

import jax
import jax.numpy as jnp
from jax import lax
from jax.experimental import pallas as pl
from jax.experimental.pallas import tpu as pltpu

F32 = jnp.float32
BF16 = jnp.bfloat16

LANES = 128
HALF = 64
A_HEADS = 4
B_HEADS = 8
PAIRS = 4
CHUNK = 64
N_BUCKETS = 32
MAX_DISTANCE = 128
LN_EPS = 1e-5
NEG_INF = -1e30
LOG2E = 1.4426950408889634
DEPTH = 1
DEEPNORM_ALPHA = (2 * DEPTH) ** 0.25
LAM_INIT = 0.8 - 0.6 * math.exp(-0.3 * 0)

T = 512
TM = 512
SKIP_THRESH = 160.0
SKIP_NORM_MARGIN = 1.05
SKIP_ABS_MARGIN = 1.0
ONES_ROWS = 16
V_ROWS = LANES + ONES_ROWS
VMEM_LIMIT = 60 * 1024 * 1024


def _ada_kernel(c_ref, w_ref, b_ref, q1_ref, k1_ref, q2_ref, k2_ref, ada_ref, lam_ref):
    c = c_ref[...]
    sc = c * jax.nn.sigmoid(c)
    ada_ref[...] = jnp.dot(sc, w_ref[...], precision=lax.Precision.HIGHEST,
                           preferred_element_type=F32) + b_ref[...]
    s1 = jnp.sum(q1_ref[...] * k1_ref[...], axis=-1, keepdims=True)
    s2 = jnp.sum(q2_ref[...] * k2_ref[...], axis=-1, keepdims=True)
    lam = jnp.exp(s1) - jnp.exp(s2) + LAM_INIT
    lam_ref[...] = jnp.broadcast_to(lam, lam_ref.shape)


def _ada_call(c8, w_ada, b_ada, q1, k1, q2, k2):
    d3 = w_ada.shape[1]
    return pl.pallas_call(
        _ada_kernel,
        out_shape=(jax.ShapeDtypeStruct((c8.shape[0], d3), F32),
                   jax.ShapeDtypeStruct((8, LANES), F32)),
        compiler_params=pltpu.CompilerParams(vmem_limit_bytes=VMEM_LIMIT),
        name="ada",
    )(c8, w_ada, b_ada, q1, k1, q2, k2)


def _t5_bucket(rel):
    nb = N_BUCKETS // 2
    ret = jnp.where(rel > 0, nb, 0)
    n = jnp.abs(rel)
    max_exact = nb // 2
    nf = jnp.maximum(n, 1).astype(F32)
    large = max_exact + (jnp.log(nf / max_exact) / math.log(MAX_DISTANCE / max_exact)
                         * (nb - max_exact)).astype(jnp.int32)
    large = jnp.minimum(large, nb - 1)
    return ret + jnp.where(n < max_exact, n, large)


def _table_kernel(rb_ref, tab_ref):
    i = pl.program_id(0)
    h = jnp.minimum(i, A_HEADS - 1)
    kl = lax.broadcasted_iota(jnp.int32, (T, T), 0)
    ql = lax.broadcasted_iota(jnp.int32, (T, T), 1)
    far = rb_ref[N_BUCKETS // 2 - 1, h]
    for d in range(2):
        rel = kl - ql + (d - 1) * T
        bucket = _t5_bucket(rel)
        bias = jnp.zeros((T, T), F32)
        for bk in range(N_BUCKETS):
            bias = jnp.where(bucket == bk, rb_ref[bk, h], bias)
        tab_a = (bias - far) * LOG2E
        if d == 1:
            tab_a = jnp.where((kl // CHUNK) <= (ql // CHUNK), tab_a, NEG_INF)
            tab_b = jnp.where(kl <= ql, 0.0, NEG_INF)
        else:
            tab_b = jnp.zeros((T, T), F32)
        tab = jnp.where(i < A_HEADS, tab_a, tab_b)
        tab_ref[0, d, :, :T] = tab
        tab_ref[0, d, :, T:] = tab


def _table_call(rel_bias):
    return pl.pallas_call(
        _table_kernel,
        grid=(A_HEADS + 1,),
        in_specs=[pl.BlockSpec(memory_space=pltpu.SMEM)],
        out_specs=pl.BlockSpec((1, 2, T, 2 * T), lambda i: (i, 0, 0, 0)),
        out_shape=jax.ShapeDtypeStruct((A_HEADS + 1, 2, T, 2 * T), F32),
        compiler_params=pltpu.CompilerParams(
            dimension_semantics=("arbitrary",), vmem_limit_bytes=VMEM_LIMIT),
        name="tables",
    )(rel_bias)


def _split3(v):
    hi = v.astype(BF16)
    r1 = v - hi.astype(F32)
    mid = r1.astype(BF16)
    lo = (r1 - mid.astype(F32)).astype(BF16)
    return hi, mid, lo


def _proj_kernel(x_ref, sh_ref, sc_ref, wrow_ref, wt_ref, bfb_ref,
                 k_ref, g_ref, qt_ref, vt_ref, e_ref, kn_ref, nc_ref, carry_ref):
    t = pl.program_id(1)
    d = x_ref.shape[2]
    x = x_ref[0]
    mu = jnp.mean(x, axis=-1, keepdims=True)
    xc = x - mu
    var = jnp.mean(xc * xc, axis=-1, keepdims=True)
    h = (xc * lax.rsqrt(var + LN_EPS)) * (1.0 + sc_ref[0]) + sh_ref[0]
    hb = h.astype(BF16)

    r = jnp.dot(hb, wrow_ref[...], preferred_element_type=F32)
    kb = r[:, :d].astype(BF16)
    k_ref[0] = kb
    g = r[:, d:2 * d]
    g_ref[0] = (g * jax.nn.sigmoid(g)).astype(BF16)

    z = r[:, 2 * d:] + bfb_ref[...]
    lane = lax.broadcasted_iota(jnp.int32, (TM, LANES), 1)
    logf = jnp.minimum(z, 0.0) - jnp.log1p(jnp.exp(-jnp.abs(z)))
    logf = jnp.where(lane < B_HEADS, logf, 0.0)
    ri = lax.broadcasted_iota(jnp.int32, (TM, TM), 0)
    ci = lax.broadcasted_iota(jnp.int32, (TM, TM), 1)
    tri = jnp.where(ci <= ri, 1.0, 0.0).astype(BF16)
    hi, mid, lo = _split3(logf)
    cs = jnp.dot(tri, jnp.concatenate([hi, mid, lo], axis=1), preferred_element_type=F32)

    @pl.when(t == 0)
    def _():
        carry_ref[...] = jnp.zeros_like(carry_ref)

    cf = (cs[:, :LANES] + cs[:, LANES:2 * LANES]) + cs[:, 2 * LANES:] + carry_ref[...]
    carry_ref[...] = cf[TM - 1:TM, :]
    ncf = -cf * LOG2E
    vh, vm, vl = _split3(ncf)

    kf = kb[:, d // 2:].astype(F32)
    gc = lax.broadcasted_iota(jnp.int32, (d // 2, LANES), 0)
    gh = lax.broadcasted_iota(jnp.int32, (d // 2, LANES), 1)
    gsel = jnp.where(gc // HALF == gh, 1.0, 0.0).astype(BF16)
    ksq = jnp.dot((kf * kf).astype(BF16), gsel, preferred_element_type=F32)
    knt = jnp.sqrt(jnp.max(ksq, axis=0, keepdims=True))
    nct = jnp.max(ncf, axis=0, keepdims=True)
    for hh in range(B_HEADS):
        kn_ref[0, hh, 0] = jnp.broadcast_to(knt[:, hh:hh + 1], (8, LANES))
        nc_ref[0, hh, 0] = jnp.broadcast_to(nct[:, hh:hh + 1], (8, LANES))

    e = jnp.where(lane < B_HEADS, vh.astype(F32),
                  jnp.where(lane < 2 * B_HEADS, pltpu.roll(vm.astype(F32), B_HEADS, 1),
                            jnp.where(lane < 3 * B_HEADS,
                                      pltpu.roll(vl.astype(F32), 2 * B_HEADS, 1), 0.0)))
    e_ref[0] = e.astype(BF16)

    tt = lax.dot_general(wt_ref[...], hb, (((1,), (1,)), ((), ())),
                         preferred_element_type=F32)
    qt_ref[0] = (tt[:d] * (LOG2E * HALF ** -0.5)).astype(BF16)
    vt_ref[0, :, 0, :LANES, :] = tt[d:].reshape(d // LANES, LANES, TM).astype(BF16)
    vt_ref[0, :, 0, LANES:, :] = jnp.ones((d // LANES, ONES_ROWS, TM), BF16)


def _proj_call(x, shift, scale, wrow, wt, bfb):
    b, s, d = x.shape
    nt = s // TM
    nrow = wrow.shape[1]
    return pl.pallas_call(
        _proj_kernel,
        grid=(b, nt),
        in_specs=[
            pl.BlockSpec((1, TM, d), lambda i, t: (i, t, 0)),
            pl.BlockSpec((1, 1, d), lambda i, t: (i, 0, 0)),
            pl.BlockSpec((1, 1, d), lambda i, t: (i, 0, 0)),
            pl.BlockSpec((d, nrow), lambda i, t: (0, 0)),
            pl.BlockSpec((2 * d, d), lambda i, t: (0, 0)),
            pl.BlockSpec((1, LANES), lambda i, t: (0, 0)),
        ],
        out_specs=(
            pl.BlockSpec((1, TM, d), lambda i, t: (i, t, 0)),
            pl.BlockSpec((1, TM, d), lambda i, t: (i, t, 0)),
            pl.BlockSpec((1, d, TM), lambda i, t: (i, 0, t)),
            pl.BlockSpec((1, d // LANES, 1, V_ROWS, TM), lambda i, t: (i, 0, t, 0, 0)),
            pl.BlockSpec((1, TM, LANES), lambda i, t: (i, t, 0)),
            pl.BlockSpec((1, B_HEADS, 1, 8, LANES), lambda i, t: (i, 0, t, 0, 0)),
            pl.BlockSpec((1, B_HEADS, 1, 8, LANES), lambda i, t: (i, 0, t, 0, 0)),
        ),
        out_shape=(
            jax.ShapeDtypeStruct((b, s, d), BF16),
            jax.ShapeDtypeStruct((b, s, d), BF16),
            jax.ShapeDtypeStruct((b, d, s), BF16),
            jax.ShapeDtypeStruct((b, d // LANES, nt, V_ROWS, TM), BF16),
            jax.ShapeDtypeStruct((b, s, LANES), BF16),
            jax.ShapeDtypeStruct((b, B_HEADS, nt, 8, LANES), F32),
            jax.ShapeDtypeStruct((b, B_HEADS, nt, 8, LANES), F32),
        ),
        scratch_shapes=[pltpu.VMEM((1, LANES), F32)],
        compiler_params=pltpu.CompilerParams(
            dimension_semantics=("arbitrary", "arbitrary"), vmem_limit_bytes=VMEM_LIMIT),
        name="proj",
    )(x, shift, scale, wrow, wt, bfb)


def _attn_kernel(*refs, is_a):
    if is_a:
        (qt_ref, k_ref, vt_ref, g_ref, tab_ref, lam_ref, ng_ref,
         y_ref, qtb_ref, m_ref, acc_ref, s_ref, mt_ref) = refs
        e_ref = None
    else:
        (qt_ref, k_ref, e_ref, vt_ref, g_ref, tab_ref, kn_ref, nc_ref,
         y_ref, qtb_ref, m_ref, acc_ref, s_ref, mt_ref) = refs
    p = pl.program_id(1)
    qi = pl.program_id(2)

    qt = qt_ref[0].astype(F32)
    row = lax.broadcasted_iota(jnp.int32, (LANES, T), 0)
    q_lo = jnp.where(row < HALF, qt, 0.0)
    q_hi = jnp.where(row >= HALF, qt, 0.0)
    qtb_ref[:LANES, :] = jnp.concatenate([q_lo, q_hi], axis=1).astype(BF16)
    if not is_a:
        r2 = lax.broadcasted_iota(jnp.int32, (LANES, 2 * T), 0)
        c2 = lax.broadcasted_iota(jnp.int32, (LANES, 2 * T), 1)
        head = 2 * p + jnp.where(c2 >= T, 1, 0)
        sel = (r2 == head) | (r2 == head + B_HEADS) | (r2 == head + 2 * B_HEADS)
        qtb_ref[LANES:, :] = jnp.where(sel, 1.0, 0.0).astype(BF16)

    m_ref[...] = jnp.full_like(m_ref, NEG_INF)
    acc_ref[...] = jnp.zeros_like(acc_ref)

    def scores(j):
        rows = pl.ds(pl.multiple_of(j * T, T), T)
        kt = k_ref[0, rows, :]
        if not is_a:
            kt = jnp.concatenate([kt, e_ref[0, rows, :]], axis=1)
        return jnp.dot(kt, qtb_ref[...], preferred_element_type=F32)

    def softmax_pv(j, s, mt):
        m_old = m_ref[...]
        m_new = jnp.maximum(m_old, mt)
        alpha = jnp.exp2(m_old - m_new)
        pb = jnp.exp2(s - m_new).astype(BF16)
        pv = jnp.dot(vt_ref[0, 0, j], pb, preferred_element_type=F32)
        acc_ref[...] = alpha * acc_ref[...] + pv
        m_ref[...] = m_new

    def tile(j, tab):
        s = scores(j)
        if tab is not None:
            s = s + tab
        softmax_pv(j, s, jnp.max(s, axis=0, keepdims=True))

    def scores_to(slot, j):
        s = scores(j)
        s_ref[slot] = s
        mt_ref[slot] = jnp.max(s, axis=0, keepdims=True)

    tile(qi, tab_ref[0, 1])
    if is_a:
        @pl.when(qi >= 1)
        def _():
            tile(qi - 1, tab_ref[0, 0])
        base = 0
        n_far = jnp.maximum(qi - 1, 0)
    else:
        need = None
        for half, qh in enumerate((q_lo, q_hi)):
            qn = jnp.sqrt(jnp.max(jnp.sum(qh * qh, axis=0, keepdims=True), axis=1, keepdims=True))
            m_min = jnp.min(m_ref[:, half * T:(half + 1) * T], axis=1, keepdims=True)
            kn = jnp.max(kn_ref[0, half], axis=0)
            bound = (SKIP_NORM_MARGIN * qn) * kn + SKIP_ABS_MARGIN
            nd = (nc_ref[0, half] + bound) > (m_min - SKIP_THRESH)
            need = nd if need is None else (need | nd)
        tiles = need.shape[0]
        jidx = lax.broadcasted_iota(jnp.int32, need.shape, 0)
        cand = jnp.where(need & (jidx < qi), jidx, qi)
        base = jnp.min(cand.reshape(tiles * 8, LANES))
        n_far = qi - base
    n_pairs = lax.shift_right_logical(n_far, 1)

    @pl.when((n_far & 1) == 1)
    def _():
        tile(base + n_far - 1, None)

    @pl.when(n_pairs > 0)
    def _():
        scores_to(0, base)

    def pair_body(i, carry):
        j = base + 2 * i
        scores_to(1, j + 1)
        softmax_pv(j, s_ref[0], mt_ref[0])
        scores_to(0, j + 2)
        softmax_pv(j + 1, s_ref[1], mt_ref[1])
        return carry

    lax.fori_loop(0, n_pairs, pair_body, 0)

    o = acc_ref[:LANES] * (1.0 / acc_ref[LANES:LANES + 1])
    if is_a:
        lam = lam_ref[0:1, 0:1]
        o2 = o[:, :T] - lam * o[:, T:]
        ms = jnp.mean(o2 * o2, axis=0, keepdims=True)
        o2 = o2 * lax.rsqrt(ms + LN_EPS) * ng_ref[...] * (1.0 - LAM_INIT)
    else:
        o2 = jnp.where(row < HALF, o[:, :T], o[:, T:])
    y = o2.T * g_ref[0].astype(F32)
    y_ref[0] = y.astype(BF16)


def _attn_call(qt, k, e, vt5, g, tab, lam, ng, kn, nc, *, is_a):
    b, d, s = qt.shape
    nq = s // T
    off = 0 if is_a else PAIRS
    kd = LANES if is_a else 2 * LANES
    in_specs = [
        pl.BlockSpec((1, LANES, T), lambda i, p, q: (i, p + off, q)),
        pl.BlockSpec((1, s, LANES), lambda i, p, q: (i, 0, p + off)),
    ]
    args = [qt, k]
    if not is_a:
        in_specs.append(pl.BlockSpec((1, s, LANES), lambda i, p, q: (i, 0, 0)))
        args.append(e)
    in_specs += [
        pl.BlockSpec((1, 1, s // TM, V_ROWS, TM), lambda i, p, q: (i, p + off, 0, 0, 0)),
        pl.BlockSpec((1, T, LANES), lambda i, p, q: (i, q, p + off)),
    ]
    args += [vt5, g]
    if is_a:
        in_specs.append(pl.BlockSpec((1, 2, T, 2 * T), lambda i, p, q: (p, 0, 0, 0)))
        in_specs.append(pl.BlockSpec((8, LANES), lambda i, p, q: (0, 0)))
        in_specs.append(pl.BlockSpec((LANES, 1), lambda i, p, q: (0, 0)))
        args += [tab, lam, ng]
    else:
        in_specs.append(pl.BlockSpec((1, 2, T, 2 * T), lambda i, p, q: (A_HEADS, 0, 0, 0)))
        stat_spec = pl.BlockSpec((1, 2, s // TM, 8, LANES), lambda i, p, q: (i, p, 0, 0, 0))
        in_specs += [stat_spec, stat_spec]
        args += [tab, kn, nc]
    return pl.pallas_call(
        functools.partial(_attn_kernel, is_a=is_a),
        grid=(b, PAIRS, nq),
        in_specs=in_specs,
        out_specs=pl.BlockSpec((1, T, LANES), lambda i, p, q: (i, q, p)),
        out_shape=jax.ShapeDtypeStruct((b, s, PAIRS * LANES), BF16),
        scratch_shapes=[
            pltpu.VMEM((kd, 2 * T), BF16),
            pltpu.VMEM((1, 2 * T), F32),
            pltpu.VMEM((V_ROWS, 2 * T), F32),
            pltpu.VMEM((2, T, 2 * T), F32),
            pltpu.VMEM((2, 1, 2 * T), F32),
        ],
        compiler_params=pltpu.CompilerParams(
            dimension_semantics=("arbitrary", "arbitrary", "arbitrary"),
            vmem_limit_bytes=VMEM_LIMIT),
        name="attn_a" if is_a else "attn_b",
    )(*args)


def _out_kernel(ya_ref, yb_ref, w1_ref, w2_ref, x_ref, gate_ref, lng_ref, lnb_ref, o_ref):
    y = (jnp.dot(ya_ref[0], w1_ref[...], preferred_element_type=F32)
         + jnp.dot(yb_ref[0], w2_ref[...], preferred_element_type=F32))
    z = DEEPNORM_ALPHA * x_ref[0] + gate_ref[0] * y
    mu = jnp.mean(z, axis=-1, keepdims=True)
    zc = z - mu
    var = jnp.mean(zc * zc, axis=-1, keepdims=True)
    o_ref[0] = zc * lax.rsqrt(var + LN_EPS) * lng_ref[...] + lnb_ref[...]


def _out_call(ya, yb, w1, w2, x, gate, lng, lnb):
    b, s, d = x.shape
    half = ya.shape[2]
    return pl.pallas_call(
        _out_kernel,
        grid=(b, s // TM),
        in_specs=[
            pl.BlockSpec((1, TM, half), lambda i, t: (i, t, 0)),
            pl.BlockSpec((1, TM, half), lambda i, t: (i, t, 0)),
            pl.BlockSpec((half, d), lambda i, t: (0, 0)),
            pl.BlockSpec((half, d), lambda i, t: (0, 0)),
            pl.BlockSpec((1, TM, d), lambda i, t: (i, t, 0)),
            pl.BlockSpec((1, 1, d), lambda i, t: (i, 0, 0)),
            pl.BlockSpec((1, d), lambda i, t: (0, 0)),
            pl.BlockSpec((1, d), lambda i, t: (0, 0)),
        ],
        out_specs=pl.BlockSpec((1, TM, d), lambda i, t: (i, t, 0)),
        out_shape=jax.ShapeDtypeStruct((b, s, d), x.dtype),
        compiler_params=pltpu.CompilerParams(
            dimension_semantics=("arbitrary", "arbitrary"), vmem_limit_bytes=VMEM_LIMIT),
        name="out",
    )(ya, yb, w1, w2, x, gate, lng, lnb)


def kernel(x, c, w_in, w_out, rel_bias, lam_q1, lam_k1, lam_q2, lam_k2, diff_norm_g,
           b_forget, w_ada, b_ada, ln_g, ln_b):
    b, s, d = x.shape
    assert s % T == 0 and T == TM and d == PAIRS * 2 * LANES
    layer = 0
    aw = PAIRS * LANES

    c8 = jnp.pad(c, ((0, 8 - b), (0, 0)))
    ada8, lam = _ada_call(c8, w_ada[layer], b_ada[layer][None], lam_q1[layer][None],
                          lam_k1[layer][None], lam_q2[layer][None], lam_k2[layer][None])
    ada = ada8[:b]
    shift = ada[:, None, :d]
    scale = ada[:, None, d:2 * d]
    gate = ada[:, None, 2 * d:]

    w = w_in[layer]
    grp = lambda i: w[:, i * aw:(i + 1) * aw]
    wbf = jnp.pad(w[:, 8 * aw:], ((0, 0), (0, LANES - B_HEADS)))
    wrow = jnp.concatenate([grp(1), grp(5), grp(3), grp(7), wbf], axis=1).astype(BF16)
    wt = jnp.concatenate([grp(0), grp(4), grp(2), grp(6)], axis=1).T.astype(BF16)
    bfb = jnp.pad(b_forget[layer], (0, LANES - B_HEADS))[None].astype(F32)

    k, g, qt, vt5, e, kn, nc = _proj_call(x, shift, scale, wrow, wt, bfb)
    tab = _table_call(rel_bias)
    ng = diff_norm_g[layer].reshape(LANES, 1)
    ya = _attn_call(qt, k, None, vt5, g, tab, lam, ng, None, None, is_a=True)
    yb = _attn_call(qt, k, e, vt5, g, tab, None, None, kn, nc, is_a=False)

    wo = w_out[layer].astype(BF16)
    return _out_call(ya, yb, wo[:aw], wo[aw:], x, gate, ln_g[layer][None], ln_b[layer][None])
```

```python
import functools
import math

import jax
import jax.numpy as jnp
from jax import lax
from jax.experimental import pallas as pl
from jax.experimental.pallas import tpu as pltpu

F32 = jnp.float32
BF16 = jnp.bfloat16

LANES = 128
HALF = 64
A_HEADS = 4
B_HEADS = 8
PAIRS = 4
CHUNK = 64
N_BUCKETS = 32
MAX_DISTANCE = 128
LN_EPS = 1e-5
NEG_INF = -1e30
LOG2E = 1.4426950408889634
DEPTH = 1
DEEPNORM_ALPHA = (2 * DEPTH) ** 0.25
LAM_INIT = 0.8 - 0.6 * math.exp(-0.3 * 0)

T = 512
TM = 512
SKIP_THRESH = 160.0
SKIP_NORM_MARGIN = 1.05
SKIP_ABS_MARGIN = 1.0
FAR_UNROLL = 4
ONES_ROWS = 16
V_ROWS = LANES + ONES_ROWS
VMEM_LIMIT = 60 * 1024 * 1024


def _ada_kernel(c_ref, w_ref, b_ref, q1_ref, k1_ref, q2_ref, k2_ref, ada_ref, lam_ref):
    c = c_ref[...]
    sc = c * jax.nn.sigmoid(c)
    ada_ref[...] = jnp.dot(sc, w_ref[...], precision=lax.Precision.HIGHEST,
                           preferred_element_type=F32) + b_ref[...]
    s1 = jnp.sum(q1_ref[...] * k1_ref[...], axis=-1, keepdims=True)
    s2 = jnp.sum(q2_ref[...] * k2_ref[...], axis=-1, keepdims=True)
    lam = jnp.exp(s1) - jnp.exp(s2) + LAM_INIT
    lam_ref[...] = jnp.broadcast_to(lam, lam_ref.shape)


def _ada_call(c8, w_ada, b_ada, q1, k1, q2, k2):
    d3 = w_ada.shape[1]
    return pl.pallas_call(
        _ada_kernel,
        out_shape=(jax.ShapeDtypeStruct((c8.shape[0], d3), F32),
                   jax.ShapeDtypeStruct((8, LANES), F32)),
        compiler_params=pltpu.CompilerParams(vmem_limit_bytes=VMEM_LIMIT),
        name="ada",
    )(c8, w_ada, b_ada, q1, k1, q2, k2)


def _t5_bucket(rel):
    nb = N_BUCKETS // 2
    ret = jnp.where(rel > 0, nb, 0)
    n = jnp.abs(rel)
    max_exact = nb // 2
    nf = jnp.maximum(n, 1).astype(F32)
    large = max_exact + (jnp.log(nf / max_exact) / math.log(MAX_DISTANCE / max_exact)
                         * (nb - max_exact)).astype(jnp.int32)
    large = jnp.minimum(large, nb - 1)
    return ret + jnp.where(n < max_exact, n, large)


def _table_kernel(rb_ref, tab_ref):
    i = pl.program_id(0)
    h = jnp.minimum(i, A_HEADS - 1)
    kl = lax.broadcasted_iota(jnp.int32, (T, T), 0)
    ql = lax.broadcasted_iota(jnp.int32, (T, T), 1)
    far = rb_ref[N_BUCKETS // 2 - 1, h]
    for d in range(2):
        rel = kl - ql + (d - 1) * T
        bucket = _t5_bucket(rel)
        bias = jnp.zeros((T, T), F32)
        for bk in range(N_BUCKETS):
            bias = jnp.where(bucket == bk, rb_ref[bk, h], bias)
        tab_a = (bias - far) * LOG2E
        if d == 1:
            tab_a = jnp.where((kl // CHUNK) <= (ql // CHUNK), tab_a, NEG_INF)
            tab_b = jnp.where(kl <= ql, 0.0, NEG_INF)
        else:
            tab_b = jnp.zeros((T, T), F32)
        tab = jnp.where(i < A_HEADS, tab_a, tab_b)
        tab_ref[0, d, :, :T] = tab
        tab_ref[0, d, :, T:] = tab


def _table_call(rel_bias):
    return pl.pallas_call(
        _table_kernel,
        grid=(A_HEADS + 1,),
        in_specs=[pl.BlockSpec(memory_space=pltpu.SMEM)],
        out_specs=pl.BlockSpec((1, 2, T, 2 * T), lambda i: (i, 0, 0, 0)),
        out_shape=jax.ShapeDtypeStruct((A_HEADS + 1, 2, T, 2 * T), F32),
        compiler_params=pltpu.CompilerParams(
            dimension_semantics=("arbitrary",), vmem_limit_bytes=VMEM_LIMIT),
        name="tables",
    )(rel_bias)


def _split3(v):
    hi = v.astype(BF16)
    r1 = v - hi.astype(F32)
    mid = r1.astype(BF16)
    lo = (r1 - mid.astype(F32)).astype(BF16)
    return hi, mid, lo


def _proj_kernel(x_ref, sh_ref, sc_ref, wrow_ref, wt_ref, bfb_ref,
                 k_ref, g_ref, qt_ref, vt_ref, e_ref, kn_ref, nc_ref, carry_ref):
    t = pl.program_id(1)
    d = x_ref.shape[2]
    x = x_ref[0]
    mu = jnp.mean(x, axis=-1, keepdims=True)
    xc = x - mu
    var = jnp.mean(xc * xc, axis=-1, keepdims=True)
    h = (xc * lax.rsqrt(var + LN_EPS)) * (1.0 + sc_ref[0]) + sh_ref[0]
    hb = h.astype(BF16)

    r = jnp.dot(hb, wrow_ref[...], preferred_element_type=F32)
    kb = r[:, :d].astype(BF16)
    k_ref[0] = kb
    g = r[:, d:2 * d]
    g_ref[0] = (g * jax.nn.sigmoid(g)).astype(BF16)

    z = r[:, 2 * d:] + bfb_ref[...]
    lane = lax.broadcasted_iota(jnp.int32, (TM, LANES), 1)
    logf = jnp.minimum(z, 0.0) - jnp.log1p(jnp.exp(-jnp.abs(z)))
    logf = jnp.where(lane < B_HEADS, logf, 0.0)
    ri = lax.broadcasted_iota(jnp.int32, (TM, TM), 0)
    ci = lax.broadcasted_iota(jnp.int32, (TM, TM), 1)
    tri = jnp.where(ci <= ri, 1.0, 0.0).astype(BF16)
    hi, mid, lo = _split3(logf)
    cs = jnp.dot(tri, jnp.concatenate([hi, mid, lo], axis=1), preferred_element_type=F32)

    @pl.when(t == 0)
    def _():
        carry_ref[...] = jnp.zeros_like(carry_ref)

    cf = (cs[:, :LANES] + cs[:, LANES:2 * LANES]) + cs[:, 2 * LANES:] + carry_ref[...]
    carry_ref[...] = cf[TM - 1:TM, :]
    ncf = -cf * LOG2E
    vh, vm, vl = _split3(ncf)

    kf = kb[:, d // 2:].astype(F32)
    gc = lax.broadcasted_iota(jnp.int32, (d // 2, LANES), 0)
    gh = lax.broadcasted_iota(jnp.int32, (d // 2, LANES), 1)
    gsel = jnp.where(gc // HALF == gh, 1.0, 0.0).astype(BF16)
    ksq = jnp.dot((kf * kf).astype(BF16), gsel, preferred_element_type=F32)
    knt = jnp.sqrt(jnp.max(ksq, axis=0, keepdims=True))
    nct = jnp.max(ncf, axis=0, keepdims=True)
    for hh in range(B_HEADS):
        kn_ref[0, hh, 0] = jnp.broadcast_to(knt[:, hh:hh + 1], (8, LANES))
        nc_ref[0, hh, 0] = jnp.broadcast_to(nct[:, hh:hh + 1], (8, LANES))

    e = jnp.where(lane < B_HEADS, vh.astype(F32),
                  jnp.where(lane < 2 * B_HEADS, pltpu.roll(vm.astype(F32), B_HEADS, 1),
                            jnp.where(lane < 3 * B_HEADS,
                                      pltpu.roll(vl.astype(F32), 2 * B_HEADS, 1), 0.0)))
    e_ref[0] = e.astype(BF16)

    tt = lax.dot_general(wt_ref[...], hb, (((1,), (1,)), ((), ())),
                         preferred_element_type=F32)
    qt_ref[0] = (tt[:d] * (LOG2E * HALF ** -0.5)).astype(BF16)
    vt_ref[0, :, 0, :LANES, :] = tt[d:].reshape(d // LANES, LANES, TM).astype(BF16)
    vt_ref[0, :, 0, LANES:, :] = jnp.ones((d // LANES, ONES_ROWS, TM), BF16)


def _proj_call(x, shift, scale, wrow, wt, bfb):
    b, s, d = x.shape
    nt = s // TM
    nrow = wrow.shape[1]
    return pl.pallas_call(
        _proj_kernel,
        grid=(b, nt),
        in_specs=[
            pl.BlockSpec((1, TM, d), lambda i, t: (i, t, 0)),
            pl.BlockSpec((1, 1, d), lambda i, t: (i, 0, 0)),
            pl.BlockSpec((1, 1, d), lambda i, t: (i, 0, 0)),
            pl.BlockSpec((d, nrow), lambda i, t: (0, 0)),
            pl.BlockSpec((2 * d, d), lambda i, t: (0, 0)),
            pl.BlockSpec((1, LANES), lambda i, t: (0, 0)),
        ],
        out_specs=(
            pl.BlockSpec((1, TM, d), lambda i, t: (i, t, 0)),
            pl.BlockSpec((1, TM, d), lambda i, t: (i, t, 0)),
            pl.BlockSpec((1, d, TM), lambda i, t: (i, 0, t)),
            pl.BlockSpec((1, d // LANES, 1, V_ROWS, TM), lambda i, t: (i, 0, t, 0, 0)),
            pl.BlockSpec((1, TM, LANES), lambda i, t: (i, t, 0)),
            pl.BlockSpec((1, B_HEADS, 1, 8, LANES), lambda i, t: (i, 0, t, 0, 0)),
            pl.BlockSpec((1, B_HEADS, 1, 8, LANES), lambda i, t: (i, 0, t, 0, 0)),
        ),
        out_shape=(
            jax.ShapeDtypeStruct((b, s, d), BF16),
            jax.ShapeDtypeStruct((b, s, d), BF16),
            jax.ShapeDtypeStruct((b, d, s), BF16),
            jax.ShapeDtypeStruct((b, d // LANES, nt, V_ROWS, TM), BF16),
            jax.ShapeDtypeStruct((b, s, LANES), BF16),
            jax.ShapeDtypeStruct((b, B_HEADS, nt, 8, LANES), F32),
            jax.ShapeDtypeStruct((b, B_HEADS, nt, 8, LANES), F32),
        ),
        scratch_shapes=[pltpu.VMEM((1, LANES), F32)],
        compiler_params=pltpu.CompilerParams(
            dimension_semantics=("arbitrary", "arbitrary"), vmem_limit_bytes=VMEM_LIMIT),
        name="proj",
    )(x, shift, scale, wrow, wt, bfb)


def _attn_kernel(*refs, is_a):
    if is_a:
        (qt_ref, k_ref, vt_ref, g_ref, tab_ref, lam_ref, ng_ref,
         y_ref, qtb_ref, m_ref, acc_ref, s_ref, mt_ref) = refs
        e_ref = None
    else:
        (qt_ref, k_ref, e_ref, vt_ref, g_ref, tab_ref, kn_ref, nc_ref,
         y_ref, qtb_ref, m_ref, acc_ref, s_ref, mt_ref) = refs
    p = pl.program_id(1)
    qi = pl.program_id(2)

    qt = qt_ref[0].astype(F32)
    row = lax.broadcasted_iota(jnp.int32, (LANES, T), 0)
    q_lo = jnp.where(row < HALF, qt, 0.0)
    q_hi = jnp.where(row >= HALF, qt, 0.0)
    qtb_ref[:LANES, :] = jnp.concatenate([q_lo, q_hi], axis=1).astype(BF16)
    if not is_a:
        r2 = lax.broadcasted_iota(jnp.int32, (LANES, 2 * T), 0)
        c2 = lax.broadcasted_iota(jnp.int32, (LANES, 2 * T), 1)
        head = 2 * p + jnp.where(c2 >= T, 1, 0)
        sel = (r2 == head) | (r2 == head + B_HEADS) | (r2 == head + 2 * B_HEADS)
        qtb_ref[LANES:, :] = jnp.where(sel, 1.0, 0.0).astype(BF16)

    m_ref[...] = jnp.full_like(m_ref, NEG_INF)
    acc_ref[...] = jnp.zeros_like(acc_ref)

    def scores(j):
        rows = pl.ds(pl.multiple_of(j * T, T), T)
        kt = k_ref[0, rows, :]
        if not is_a:
            kt = jnp.concatenate([kt, e_ref[0, rows, :]], axis=1)
        return jnp.dot(kt, qtb_ref[...], preferred_element_type=F32)

    def softmax_pv(vt, s, mt):
        m_old = m_ref[...]
        m_new = jnp.maximum(m_old, mt)
        alpha = jnp.exp2(m_old - m_new)
        pb = jnp.exp2(s - m_new).astype(BF16)
        pv = jnp.dot(vt, pb, preferred_element_type=F32)
        acc_ref[...] = alpha * acc_ref[...] + pv
        m_ref[...] = m_new

    def v_tile(j, valid=None):
        vt = vt_ref[0, 0, jnp.maximum(j, 0)]
        return vt if valid is None else jnp.where(valid, vt, jnp.zeros_like(vt))

    def scores_to(slot, j):
        s = scores(jnp.maximum(j, 0))
        s_ref[slot] = s
        mt_ref[slot] = jnp.max(s, axis=0, keepdims=True)

    s_diag = scores(qi) + tab_ref[0, 1]
    if is_a:
        s_near = scores(jnp.maximum(qi - 1, 0)) + tab_ref[0, 0]
        softmax_pv(v_tile(qi), s_diag, jnp.max(s_diag, axis=0, keepdims=True))
        top = qi - 2
        scores_to(0, top)
        softmax_pv(v_tile(qi - 1, qi >= 1), s_near, jnp.max(s_near, axis=0, keepdims=True))
        n_far = jnp.maximum(qi - 1, 0)
    else:
        top = qi - 1
        scores_to(0, top)
        softmax_pv(v_tile(qi), s_diag, jnp.max(s_diag, axis=0, keepdims=True))
        need = None
        for half, qh in enumerate((q_lo, q_hi)):
            qn = jnp.sqrt(jnp.max(jnp.sum(qh * qh, axis=0, keepdims=True), axis=1, keepdims=True))
            m_min = jnp.min(m_ref[:, half * T:(half + 1) * T], axis=1, keepdims=True)
            kn = jnp.max(kn_ref[0, half], axis=0)
            bound = (SKIP_NORM_MARGIN * qn) * kn + SKIP_ABS_MARGIN
            nd = (nc_ref[0, half] + bound) > (m_min - SKIP_THRESH)
            need = nd if need is None else (need | nd)
        tiles = need.shape[0]
        jidx = lax.broadcasted_iota(jnp.int32, need.shape, 0)
        cand = jnp.where(need & (jidx < qi), jidx, qi)
        n_far = qi - jnp.min(cand.reshape(tiles * 8, LANES))

    def far_steps(t0, count):
        for u in range(count):
            t = t0 + u
            scores_to((u + 1) % 2, top - t - 1)
            valid = None if u % 2 == 0 else t < n_far
            softmax_pv(v_tile(top - t, valid), s_ref[u % 2], mt_ref[u % 2])

    n_pad = n_far + (n_far & 1)
    n_long = n_pad // FAR_UNROLL

    def long_body(i, carry):
        far_steps(FAR_UNROLL * i, FAR_UNROLL)
        return carry

    def pair_body(i, carry):
        far_steps(FAR_UNROLL * n_long + 2 * i, 2)
        return carry

    lax.fori_loop(0, n_long, long_body, 0)
    lax.fori_loop(0, (n_pad - FAR_UNROLL * n_long) // 2, pair_body, 0)

    o = acc_ref[:LANES] * (1.0 / acc_ref[LANES:LANES + 1])
    if is_a:
        lam = lam_ref[0:1, 0:1]
        o2 = o[:, :T] - lam * o[:, T:]
        ms = jnp.mean(o2 * o2, axis=0, keepdims=True)
        o2 = o2 * lax.rsqrt(ms + LN_EPS) * ng_ref[...] * (1.0 - LAM_INIT)
    else:
        o2 = jnp.where(row < HALF, o[:, :T], o[:, T:])
    y = o2.T * g_ref[0].astype(F32)
    y_ref[0] = y.astype(BF16)


def _attn_call(qt, k, e, vt5, g, tab, lam, ng, kn, nc, *, is_a):
    b, d, s = qt.shape
    nq = s // T
    off = 0 if is_a else PAIRS
    kd = LANES if is_a else 2 * LANES
    in_specs = [
        pl.BlockSpec((1, LANES, T), lambda i, p, q: (i, p + off, q)),
        pl.BlockSpec((1, s, LANES), lambda i, p, q: (i, 0, p + off)),
    ]
    args = [qt, k]
    if not is_a:
        in_specs.append(pl.BlockSpec((1, s, LANES), lambda i, p, q: (i, 0, 0)))
        args.append(e)
    in_specs += [
        pl.BlockSpec((1, 1, s // TM, V_ROWS, TM), lambda i, p, q: (i, p + off, 0, 0, 0)),
        pl.BlockSpec((1, T, LANES), lambda i, p, q: (i, q, p + off)),
    ]
    args += [vt5, g]
    if is_a:
        in_specs.append(pl.BlockSpec((1, 2, T, 2 * T), lambda i, p, q: (p, 0, 0, 0)))
        in_specs.append(pl.BlockSpec((8, LANES), lambda i, p, q: (0, 0)))
        in_specs.append(pl.BlockSpec((LANES, 1), lambda i, p, q: (0, 0)))
        args += [tab, lam, ng]
    else:
        in_specs.append(pl.BlockSpec((1, 2, T, 2 * T), lambda i, p, q: (A_HEADS, 0, 0, 0)))
        stat_spec = pl.BlockSpec((1, 2, s // TM, 8, LANES), lambda i, p, q: (i, p, 0, 0, 0))
        in_specs += [stat_spec, stat_spec]
        args += [tab, kn, nc]
    return pl.pallas_call(
        functools.partial(_attn_kernel, is_a=is_a),
        grid=(b, PAIRS, nq),
        in_specs=in_specs,
        out_specs=pl.BlockSpec((1, T, LANES), lambda i, p, q: (i, q, p)),
        out_shape=jax.ShapeDtypeStruct((b, s, PAIRS * LANES), BF16),
        scratch_shapes=[
            pltpu.VMEM((kd, 2 * T), BF16),
            pltpu.VMEM((1, 2 * T), F32),
            pltpu.VMEM((V_ROWS, 2 * T), F32),
            pltpu.VMEM((2, T, 2 * T), F32),
            pltpu.VMEM((2, 1, 2 * T), F32),
        ],
        compiler_params=pltpu.CompilerParams(
            dimension_semantics=("arbitrary", "arbitrary", "arbitrary"),
            vmem_limit_bytes=VMEM_LIMIT),
        name="attn_a" if is_a else "attn_b",
    )(*args)


def _out_kernel(ya_ref, yb_ref, w1_ref, w2_ref, x_ref, gate_ref, lng_ref, lnb_ref, o_ref):
    y = (jnp.dot(ya_ref[0], w1_ref[...], preferred_element_type=F32)
         + jnp.dot(yb_ref[0], w2_ref[...], preferred_element_type=F32))
    z = DEEPNORM_ALPHA * x_ref[0] + gate_ref[0] * y
    mu = jnp.mean(z, axis=-1, keepdims=True)
    zc = z - mu
    var = jnp.mean(zc * zc, axis=-1, keepdims=True)
    o_ref[0] = zc * lax.rsqrt(var + LN_EPS) * lng_ref[...] + lnb_ref[...]


def _out_call(ya, yb, w1, w2, x, gate, lng, lnb):
    b, s, d = x.shape
    half = ya.shape[2]
    return pl.pallas_call(
        _out_kernel,
        grid=(b, s // TM),
        in_specs=[
            pl.BlockSpec((1, TM, half), lambda i, t: (i, t, 0)),
            pl.BlockSpec((1, TM, half), lambda i, t: (i, t, 0)),
            pl.BlockSpec((half, d), lambda i, t: (0, 0)),
            pl.BlockSpec((half, d), lambda i, t: (0, 0)),
            pl.BlockSpec((1, TM, d), lambda i, t: (i, t, 0)),
            pl.BlockSpec((1, 1, d), lambda i, t: (i, 0, 0)),
            pl.BlockSpec((1, d), lambda i, t: (0, 0)),
            pl.BlockSpec((1, d), lambda i, t: (0, 0)),
        ],
        out_specs=pl.BlockSpec((1, TM, d), lambda i, t: (i, t, 0)),
        out_shape=jax.ShapeDtypeStruct((b, s, d), x.dtype),
        compiler_params=pltpu.CompilerParams(
            dimension_semantics=("arbitrary", "arbitrary"), vmem_limit_bytes=VMEM_LIMIT),
        name="out",
    )(ya, yb, w1, w2, x, gate, lng, lnb)


def kernel(x, c, w_in, w_out, rel_bias, lam_q1, lam_k1, lam_q2, lam_k2, diff_norm_g,
           b_forget, w_ada, b_ada, ln_g, ln_b):
    b, s, d = x.shape
    assert s % T == 0 and T == TM and d == PAIRS * 2 * LANES
    layer = 0
    aw = PAIRS * LANES

    c8 = jnp.pad(c, ((0, 8 - b), (0, 0)))
    ada8, lam = _ada_call(c8, w_ada[layer], b_ada[layer][None], lam_q1[layer][None],
                          lam_k1[layer][None], lam_q2[layer][None], lam_k2[layer][None])
    ada = ada8[:b]
    shift = ada[:, None, :d]
    scale = ada[:, None, d:2 * d]
    gate = ada[:, None, 2 * d:]

    w = w_in[layer]
    grp = lambda i: w[:, i * aw:(i + 1) * aw]
    wbf = jnp.pad(w[:, 8 * aw:], ((0, 0), (0, LANES - B_HEADS)))
    wrow = jnp.concatenate([grp(1), grp(5), grp(3), grp(7), wbf], axis=1).astype(BF16)
    wt = jnp.concatenate([grp(0), grp(4), grp(2), grp(6)], axis=1).T.astype(BF16)
    bfb = jnp.pad(b_forget[layer], (0, LANES - B_HEADS))[None].astype(F32)

    k, g, qt, vt5, e, kn, nc = _proj_call(x, shift, scale, wrow, wt, bfb)
    tab = _table_call(rel_bias)
    ng = diff_norm_g[layer].reshape(LANES, 1)
    ya = _attn_call(qt, k, None, vt5, g, tab, lam, ng, None, None, is_a=True)
    yb = _attn_call(qt, k, e, vt5, g, tab, None, None, kn, nc, is_a=False)

    wo = w_out[layer].astype(BF16)
    return _out_call(ya, yb, wo[:aw], wo[aw:], x, gate, ln_g[layer][None], ln_b[layer][None])
```

```python
import functools
import math

import jax
import jax.numpy as jnp
from jax import lax
from jax.experimental import pallas as pl
from jax.experimental.pallas import tpu as pltpu

F32 = jnp.float32
BF16 = jnp.bfloat16

LANES = 128
HALF = 64
A_HEADS = 4
B_HEADS = 8
PAIRS = 4
CHUNK = 64
N_BUCKETS = 32
MAX_DISTANCE = 128
LN_EPS = 1e-5
NEG_INF = -1e30
LOG2E = 1.4426950408889634
DEPTH = 1
DEEPNORM_ALPHA = (2 * DEPTH) ** 0.25
LAM_INIT = 0.8 - 0.6 * math.exp(-0.3 * 0)

T = 512
TM = 512
SKIP_THRESH = 160.0
SKIP_NORM_MARGIN = 1.05
SKIP_ABS_MARGIN = 1.0
FAR_UNROLLS = (8, 4, 2)
ONES_ROWS = 16
VA_ROWS = LANES + ONES_ROWS
VB_ROWS = HALF + ONES_ROWS
VMEM_LIMIT = 60 * 1024 * 1024


def _ada_kernel(c_ref, w_ref, b_ref, q1_ref, k1_ref, q2_ref, k2_ref, ada_ref, lam_ref):
    c = c_ref[...]
    sc = c * jax.nn.sigmoid(c)
    ada_ref[...] = jnp.dot(sc, w_ref[...], precision=lax.Precision.HIGHEST,
                           preferred_element_type=F32) + b_ref[...]
    s1 = jnp.sum(q1_ref[...] * k1_ref[...], axis=-1, keepdims=True)
    s2 = jnp.sum(q2_ref[...] * k2_ref[...], axis=-1, keepdims=True)
    lam = jnp.exp(s1) - jnp.exp(s2) + LAM_INIT
    lam_ref[...] = jnp.broadcast_to(lam, lam_ref.shape)


def _ada_call(c8, w_ada, b_ada, q1, k1, q2, k2):
    d3 = w_ada.shape[1]
    return pl.pallas_call(
        _ada_kernel,
        out_shape=(jax.ShapeDtypeStruct((c8.shape[0], d3), F32),
                   jax.ShapeDtypeStruct((8, LANES), F32)),
        compiler_params=pltpu.CompilerParams(vmem_limit_bytes=VMEM_LIMIT),
        name="ada",
    )(c8, w_ada, b_ada, q1, k1, q2, k2)


def _t5_bucket(rel):
    nb = N_BUCKETS // 2
    ret = jnp.where(rel > 0, nb, 0)
    n = jnp.abs(rel)
    max_exact = nb // 2
    nf = jnp.maximum(n, 1).astype(F32)
    large = max_exact + (jnp.log(nf / max_exact) / math.log(MAX_DISTANCE / max_exact)
                         * (nb - max_exact)).astype(jnp.int32)
    large = jnp.minimum(large, nb - 1)
    return ret + jnp.where(n < max_exact, n, large)


def _table_kernel(rb_ref, tab_ref):
    i = pl.program_id(0)
    h = jnp.minimum(i, A_HEADS - 1)
    kl = lax.broadcasted_iota(jnp.int32, (T, T), 0)
    ql = lax.broadcasted_iota(jnp.int32, (T, T), 1)
    far = rb_ref[N_BUCKETS // 2 - 1, h]
    for d in range(2):
        rel = kl - ql + (d - 1) * T
        bucket = _t5_bucket(rel)
        bias = jnp.zeros((T, T), F32)
        for bk in range(N_BUCKETS):
            bias = jnp.where(bucket == bk, rb_ref[bk, h], bias)
        tab_a = (bias - far) * LOG2E
        if d == 1:
            tab_a = jnp.where((kl // CHUNK) <= (ql // CHUNK), tab_a, NEG_INF)
            tab_b = jnp.where(kl <= ql, 0.0, NEG_INF)
        else:
            tab_b = jnp.zeros((T, T), F32)
        tab = jnp.where(i < A_HEADS, tab_a, tab_b)
        tab_ref[0, d, :, :T] = tab
        tab_ref[0, d, :, T:] = tab


def _table_call(rel_bias):
    return pl.pallas_call(
        _table_kernel,
        grid=(A_HEADS + 1,),
        in_specs=[pl.BlockSpec(memory_space=pltpu.SMEM)],
        out_specs=pl.BlockSpec((1, 2, T, 2 * T), lambda i: (i, 0, 0, 0)),
        out_shape=jax.ShapeDtypeStruct((A_HEADS + 1, 2, T, 2 * T), F32),
        compiler_params=pltpu.CompilerParams(
            dimension_semantics=("arbitrary",), vmem_limit_bytes=VMEM_LIMIT),
        name="tables",
    )(rel_bias)


def _split3(v):
    hi = v.astype(BF16)
    r1 = v - hi.astype(F32)
    mid = r1.astype(BF16)
    lo = (r1 - mid.astype(F32)).astype(BF16)
    return hi, mid, lo


def _proj_kernel(x_ref, sh_ref, sc_ref, wrow_ref, wt_ref, bfb_ref,
                 k_ref, g_ref, qt_ref, vta_ref, vtb_ref, e_ref, kn_ref, nc_ref, carry_ref):
    t = pl.program_id(1)
    d = x_ref.shape[2]
    x = x_ref[0]
    mu = jnp.mean(x, axis=-1, keepdims=True)
    xc = x - mu
    var = jnp.mean(xc * xc, axis=-1, keepdims=True)
    h = (xc * lax.rsqrt(var + LN_EPS)) * (1.0 + sc_ref[0]) + sh_ref[0]
    hb = h.astype(BF16)

    r = jnp.dot(hb, wrow_ref[...], preferred_element_type=F32)
    kb = r[:, :d].astype(BF16)
    k_ref[0] = kb
    g = r[:, d:2 * d]
    g_ref[0] = (g * jax.nn.sigmoid(g)).astype(BF16)

    z = r[:, 2 * d:] + bfb_ref[...]
    lane = lax.broadcasted_iota(jnp.int32, (TM, LANES), 1)
    logf = jnp.minimum(z, 0.0) - jnp.log1p(jnp.exp(-jnp.abs(z)))
    logf = jnp.where(lane < B_HEADS, logf, 0.0)
    ri = lax.broadcasted_iota(jnp.int32, (TM, TM), 0)
    ci = lax.broadcasted_iota(jnp.int32, (TM, TM), 1)
    tri = jnp.where(ci <= ri, 1.0, 0.0).astype(BF16)
    hi, mid, lo = _split3(logf)
    cs = jnp.dot(tri, jnp.concatenate([hi, mid, lo], axis=1), preferred_element_type=F32)

    @pl.when(t == 0)
    def _():
        carry_ref[...] = jnp.zeros_like(carry_ref)

    cf = (cs[:, :LANES] + cs[:, LANES:2 * LANES]) + cs[:, 2 * LANES:] + carry_ref[...]
    carry_ref[...] = cf[TM - 1:TM, :]
    ncf = -cf * LOG2E
    vh, vm, vl = _split3(ncf)

    kf = kb[:, d // 2:].astype(F32)
    gc = lax.broadcasted_iota(jnp.int32, (d // 2, LANES), 0)
    gh = lax.broadcasted_iota(jnp.int32, (d // 2, LANES), 1)
    gsel = jnp.where(gc // HALF == gh, 1.0, 0.0).astype(BF16)
    ksq = jnp.dot((kf * kf).astype(BF16), gsel, preferred_element_type=F32)
    knt = jnp.sqrt(jnp.max(ksq, axis=0, keepdims=True))
    nct = jnp.max(ncf, axis=0, keepdims=True)
    for hh in range(B_HEADS):
        kn_ref[0, hh, 0] = jnp.broadcast_to(knt[:, hh:hh + 1], (8, LANES))
        nc_ref[0, hh, 0] = jnp.broadcast_to(nct[:, hh:hh + 1], (8, LANES))

    e = jnp.where(lane < B_HEADS, vh.astype(F32),
                  jnp.where(lane < 2 * B_HEADS, pltpu.roll(vm.astype(F32), B_HEADS, 1),
                            jnp.where(lane < 3 * B_HEADS,
                                      pltpu.roll(vl.astype(F32), 2 * B_HEADS, 1), 0.0)))
    e_ref[0] = e.astype(BF16)

    tt = lax.dot_general(wt_ref[...], hb, (((1,), (1,)), ((), ())),
                         preferred_element_type=F32)
    qt_ref[0] = (tt[:d] * (LOG2E * HALF ** -0.5)).astype(BF16)
    hd = d // 2
    vta_ref[0, :, 0, :LANES, :] = tt[d:d + hd].reshape(A_HEADS, LANES, TM).astype(BF16)
    vtb_ref[0, :, 0, :HALF, :] = tt[d + hd:].reshape(B_HEADS, HALF, TM).astype(BF16)
    vta_ref[0, :, 0, LANES:, :] = jnp.ones((A_HEADS, ONES_ROWS, TM), BF16)
    vtb_ref[0, :, 0, HALF:, :] = jnp.ones((B_HEADS, ONES_ROWS, TM), BF16)


def _proj_call(x, shift, scale, wrow, wt, bfb):
    b, s, d = x.shape
    nt = s // TM
    nrow = wrow.shape[1]
    return pl.pallas_call(
        _proj_kernel,
        grid=(b, nt),
        in_specs=[
            pl.BlockSpec((1, TM, d), lambda i, t: (i, t, 0)),
            pl.BlockSpec((1, 1, d), lambda i, t: (i, 0, 0)),
            pl.BlockSpec((1, 1, d), lambda i, t: (i, 0, 0)),
            pl.BlockSpec((d, nrow), lambda i, t: (0, 0)),
            pl.BlockSpec((2 * d, d), lambda i, t: (0, 0)),
            pl.BlockSpec((1, LANES), lambda i, t: (0, 0)),
        ],
        out_specs=(
            pl.BlockSpec((1, TM, d), lambda i, t: (i, t, 0)),
            pl.BlockSpec((1, TM, d), lambda i, t: (i, t, 0)),
            pl.BlockSpec((1, d, TM), lambda i, t: (i, 0, t)),
            pl.BlockSpec((1, A_HEADS, 1, VA_ROWS, TM), lambda i, t: (i, 0, t, 0, 0)),
            pl.BlockSpec((1, B_HEADS, 1, VB_ROWS, TM), lambda i, t: (i, 0, t, 0, 0)),
            pl.BlockSpec((1, TM, LANES), lambda i, t: (i, t, 0)),
            pl.BlockSpec((1, B_HEADS, 1, 8, LANES), lambda i, t: (i, 0, t, 0, 0)),
            pl.BlockSpec((1, B_HEADS, 1, 8, LANES), lambda i, t: (i, 0, t, 0, 0)),
        ),
        out_shape=(
            jax.ShapeDtypeStruct((b, s, d), BF16),
            jax.ShapeDtypeStruct((b, s, d), BF16),
            jax.ShapeDtypeStruct((b, d, s), BF16),
            jax.ShapeDtypeStruct((b, A_HEADS, nt, VA_ROWS, TM), BF16),
            jax.ShapeDtypeStruct((b, B_HEADS, nt, VB_ROWS, TM), BF16),
            jax.ShapeDtypeStruct((b, s, LANES), BF16),
            jax.ShapeDtypeStruct((b, B_HEADS, nt, 8, LANES), F32),
            jax.ShapeDtypeStruct((b, B_HEADS, nt, 8, LANES), F32),
        ),
        scratch_shapes=[pltpu.VMEM((1, LANES), F32)],
        compiler_params=pltpu.CompilerParams(
            dimension_semantics=("arbitrary", "arbitrary"), vmem_limit_bytes=VMEM_LIMIT),
        name="proj",
    )(x, shift, scale, wrow, wt, bfb)


def _attn_kernel(*refs, is_a):
    if is_a:
        (qt_ref, k_ref, vt_ref, g_ref, tab_ref, lam_ref, ng_ref,
         y_ref, qtb_ref, m_ref, acc_ref, s_ref, mt_ref) = refs
        e_ref = None
    else:
        (qt_ref, k_ref, e_ref, vt_ref, g_ref, tab_ref, kn_ref, nc_ref,
         y_ref, qtb_ref, m_ref, acc_ref, s_ref, mt_ref) = refs
    p = pl.program_id(1)
    qi = pl.program_id(2)

    qt = qt_ref[0].astype(F32)
    row = lax.broadcasted_iota(jnp.int32, (LANES, T), 0)
    q_lo = jnp.where(row < HALF, qt, 0.0)
    q_hi = jnp.where(row >= HALF, qt, 0.0)
    qtb_ref[:LANES, :] = jnp.concatenate([q_lo, q_hi], axis=1).astype(BF16)
    if not is_a:
        r2 = lax.broadcasted_iota(jnp.int32, (LANES, 2 * T), 0)
        c2 = lax.broadcasted_iota(jnp.int32, (LANES, 2 * T), 1)
        head = 2 * p + jnp.where(c2 >= T, 1, 0)
        sel = (r2 == head) | (r2 == head + B_HEADS) | (r2 == head + 2 * B_HEADS)
        qtb_ref[LANES:, :] = jnp.where(sel, 1.0, 0.0).astype(BF16)

    def scores(j):
        rows = pl.ds(pl.multiple_of(j * T, T), T)
        kt = k_ref[0, rows, :]
        if not is_a:
            kt = jnp.concatenate([kt, e_ref[0, rows, :]], axis=1)
        return jnp.dot(kt, qtb_ref[...], preferred_element_type=F32)

    def pv_dot(vts, pb):
        if is_a:
            return jnp.dot(vts[0], pb, preferred_element_type=F32)
        return jnp.concatenate(
            [jnp.dot(vts[0], pb[:, :T], preferred_element_type=F32),
             jnp.dot(vts[1], pb[:, T:], preferred_element_type=F32)], axis=1)

    def softmax_pv(vts, s, mt, first=False):
        if first:
            m_new = mt
            acc_ref[...] = pv_dot(vts, jnp.exp2(s - m_new).astype(BF16))
        else:
            m_old = m_ref[...]
            m_new = jnp.maximum(m_old, mt)
            alpha = jnp.exp2(m_old - m_new)
            pv = pv_dot(vts, jnp.exp2(s - m_new).astype(BF16))
            acc_ref[...] = alpha * acc_ref[...] + pv
        m_ref[...] = m_new

    def v_tile(j, valid=None):
        jc = jnp.maximum(j, 0)
        vts = [vt_ref[0, u, jc] for u in range(1 if is_a else 2)]
        if valid is not None:
            vts = [jnp.where(valid, vt, jnp.zeros_like(vt)) for vt in vts]
        return vts

    def scores_to(slot, j):
        s = scores(jnp.maximum(j, 0))
        s_ref[slot] = s
        mt_ref[slot] = jnp.max(s, axis=0, keepdims=True)

    s_diag = scores(qi) + tab_ref[0, 1]
    if is_a:
        s_near = scores(jnp.maximum(qi - 1, 0)) + tab_ref[0, 0]
        softmax_pv(v_tile(qi), s_diag, jnp.max(s_diag, axis=0, keepdims=True), first=True)
        top = qi - 2
        scores_to(0, top)
        softmax_pv(v_tile(qi - 1, qi >= 1), s_near, jnp.max(s_near, axis=0, keepdims=True))
        n_far = jnp.maximum(qi - 1, 0)
    else:
        top = qi - 1
        scores_to(0, top)
        softmax_pv(v_tile(qi), s_diag, jnp.max(s_diag, axis=0, keepdims=True), first=True)
        need = None
        for half, qh in enumerate((q_lo, q_hi)):
            qn = jnp.sqrt(jnp.max(jnp.sum(qh * qh, axis=0, keepdims=True), axis=1, keepdims=True))
            m_min = jnp.min(m_ref[:, half * T:(half + 1) * T], axis=1, keepdims=True)
            kn = jnp.max(kn_ref[0, half], axis=0)
            bound = (SKIP_NORM_MARGIN * qn) * kn + SKIP_ABS_MARGIN
            nd = (nc_ref[0, half] + bound) > (m_min - SKIP_THRESH)
            need = nd if need is None else (need | nd)
        tiles = need.shape[0]
        jidx = lax.broadcasted_iota(jnp.int32, need.shape, 0)
        cand = jnp.where(need & (jidx < qi), jidx, qi)
        n_far = qi - jnp.min(cand.reshape(tiles * 8, LANES))

    def far_steps(t0, count):
        for u in range(count):
            t = t0 + u
            scores_to((u + 1) % 2, top - t - 1)
            valid = None if u % 2 == 0 else t < n_far
            softmax_pv(v_tile(top - t, valid), s_ref[u % 2], mt_ref[u % 2])

    done = 0
    left = n_far + (n_far & 1)
    for unroll in FAR_UNROLLS:
        trips = left // unroll

        def body(i, carry, unroll=unroll, done=done):
            far_steps(done + unroll * i, unroll)
            return carry

        lax.fori_loop(0, trips, body, 0)
        done = done + unroll * trips
        left = left - unroll * trips

    vd = acc_ref.shape[0] - ONES_ROWS
    o = acc_ref[:vd] * (1.0 / acc_ref[vd:vd + 1])
    if is_a:
        lam = lam_ref[0:1, 0:1]
        o2 = o[:, :T] - lam * o[:, T:]
        ms = jnp.mean(o2 * o2, axis=0, keepdims=True)
        o2 = o2 * lax.rsqrt(ms + LN_EPS) * ng_ref[...] * (1.0 - LAM_INIT)
    else:
        o2 = jnp.concatenate([o[:, :T], o[:, T:]], axis=0)
    y = o2.T * g_ref[0].astype(F32)
    y_ref[0] = y.astype(BF16)


def _attn_call(qt, k, e, vt5, g, tab, lam, ng, kn, nc, *, is_a):
    b, d, s = qt.shape
    nq = s // T
    off = 0 if is_a else PAIRS
    kd = LANES if is_a else 2 * LANES
    v_rows = vt5.shape[3]
    in_specs = [
        pl.BlockSpec((1, LANES, T), lambda i, p, q: (i, p + off, q)),
        pl.BlockSpec((1, s, LANES), lambda i, p, q: (i, 0, p + off)),
    ]
    args = [qt, k]
    if not is_a:
        in_specs.append(pl.BlockSpec((1, s, LANES), lambda i, p, q: (i, 0, 0)))
        args.append(e)
    in_specs += [
        pl.BlockSpec((1, 1 if is_a else 2, s // TM, v_rows, TM), lambda i, p, q: (i, p, 0, 0, 0)),
        pl.BlockSpec((1, T, LANES), lambda i, p, q: (i, q, p + off)),
    ]
    args += [vt5, g]
    if is_a:
        in_specs.append(pl.BlockSpec((1, 2, T, 2 * T), lambda i, p, q: (p, 0, 0, 0)))
        in_specs.append(pl.BlockSpec((8, LANES), lambda i, p, q: (0, 0)))
        in_specs.append(pl.BlockSpec((LANES, 1), lambda i, p, q: (0, 0)))
        args += [tab, lam, ng]
    else:
        in_specs.append(pl.BlockSpec((1, 2, T, 2 * T), lambda i, p, q: (A_HEADS, 0, 0, 0)))
        stat_spec = pl.BlockSpec((1, 2, s // TM, 8, LANES), lambda i, p, q: (i, p, 0, 0, 0))
        in_specs += [stat_spec, stat_spec]
        args += [tab, kn, nc]
    return pl.pallas_call(
        functools.partial(_attn_kernel, is_a=is_a),
        grid=(b, PAIRS, nq),
        in_specs=in_specs,
        out_specs=pl.BlockSpec((1, T, LANES), lambda i, p, q: (i, q, p)),
        out_shape=jax.ShapeDtypeStruct((b, s, PAIRS * LANES), BF16),
        scratch_shapes=[
            pltpu.VMEM((kd, 2 * T), BF16),
            pltpu.VMEM((1, 2 * T), F32),
            pltpu.VMEM((v_rows, 2 * T), F32),
            pltpu.VMEM((2, T, 2 * T), F32),
            pltpu.VMEM((2, 1, 2 * T), F32),
        ],
        compiler_params=pltpu.CompilerParams(
            dimension_semantics=("arbitrary", "arbitrary", "arbitrary"),
            vmem_limit_bytes=VMEM_LIMIT),
        name="attn_a" if is_a else "attn_b",
    )(*args)


def _out_kernel(ya_ref, yb_ref, w1_ref, w2_ref, x_ref, gate_ref, lng_ref, lnb_ref, o_ref):
    y = (jnp.dot(ya_ref[0], w1_ref[...], preferred_element_type=F32)
         + jnp.dot(yb_ref[0], w2_ref[...], preferred_element_type=F32))
    z = DEEPNORM_ALPHA * x_ref[0] + gate_ref[0] * y
    mu = jnp.mean(z, axis=-1, keepdims=True)
    zc = z - mu
    var = jnp.mean(zc * zc, axis=-1, keepdims=True)
    o_ref[0] = zc * lax.rsqrt(var + LN_EPS) * lng_ref[...] + lnb_ref[...]


def _out_call(ya, yb, w1, w2, x, gate, lng, lnb):
    b, s, d = x.shape
    half = ya.shape[2]
    return pl.pallas_call(
        _out_kernel,
        grid=(b, s // TM),
        in_specs=[
            pl.BlockSpec((1, TM, half), lambda i, t: (i, t, 0)),
            pl.BlockSpec((1, TM, half), lambda i, t: (i, t, 0)),
            pl.BlockSpec((half, d), lambda i, t: (0, 0)),
            pl.BlockSpec((half, d), lambda i, t: (0, 0)),
            pl.BlockSpec((1, TM, d), lambda i, t: (i, t, 0)),
            pl.BlockSpec((1, 1, d), lambda i, t: (i, 0, 0)),
            pl.BlockSpec((1, d), lambda i, t: (0, 0)),
            pl.BlockSpec((1, d), lambda i, t: (0, 0)),
        ],
        out_specs=pl.BlockSpec((1, TM, d), lambda i, t: (i, t, 0)),
        out_shape=jax.ShapeDtypeStruct((b, s, d), x.dtype),
        compiler_params=pltpu.CompilerParams(
            dimension_semantics=("arbitrary", "arbitrary"), vmem_limit_bytes=VMEM_LIMIT),
        name="out",
    )(ya, yb, w1, w2, x, gate, lng, lnb)


def kernel(x, c, w_in, w_out, rel_bias, lam_q1, lam_k1, lam_q2, lam_k2, diff_norm_g,
           b_forget, w_ada, b_ada, ln_g, ln_b):
    b, s, d = x.shape
    assert s % T == 0 and T == TM and d == PAIRS * 2 * LANES
    layer = 0
    aw = PAIRS * LANES

    c8 = jnp.pad(c, ((0, 8 - b), (0, 0)))
    ada8, lam = _ada_call(c8, w_ada[layer], b_ada[layer][None], lam_q1[layer][None],
                          lam_k1[layer][None], lam_q2[layer][None], lam_k2[layer][None])
    ada = ada8[:b]
    shift = ada[:, None, :d]
    scale = ada[:, None, d:2 * d]
    gate = ada[:, None, 2 * d:]

    w = w_in[layer]
    grp = lambda i: w[:, i * aw:(i + 1) * aw]
    wbf = jnp.pad(w[:, 8 * aw:], ((0, 0), (0, LANES - B_HEADS)))
    wrow = jnp.concatenate([grp(1), grp(5), grp(3), grp(7), wbf], axis=1).astype(BF16)
    wt = jnp.concatenate([grp(0), grp(4), grp(2), grp(6)], axis=1).T.astype(BF16)
    bfb = jnp.pad(b_forget[layer], (0, LANES - B_HEADS))[None].astype(F32)

    k, g, qt, vta, vtb, e, kn, nc = _proj_call(x, shift, scale, wrow, wt, bfb)
    tab = _table_call(rel_bias)
    ng = diff_norm_g[layer].reshape(LANES, 1)
    ya = _attn_call(qt, k, None, vta, g, tab, lam, ng, None, None, is_a=True)
    yb = _attn_call(qt, k, e, vtb, g, tab, None, None, kn, nc, is_a=False)

    wo = w_out[layer].astype(BF16)
    return _out_call(ya, yb, wo[:aw], wo[aw:], x, gate, ln_g[layer][None], ln_b[layer][None])
```

```python
import functools
import math

import jax
import jax.numpy as jnp
from jax import lax
from jax.experimental import pallas as pl
from jax.experimental.pallas import tpu as pltpu

F32 = jnp.float32
BF16 = jnp.bfloat16

LANES = 128
HALF = 64
A_HEADS = 4
B_HEADS = 8
PAIRS = 4
CHUNK = 64
N_BUCKETS = 32
MAX_DISTANCE = 128
LN_EPS = 1e-5
NEG_INF = -1e30
LOG2E = 1.4426950408889634
DEPTH = 1
DEEPNORM_ALPHA = (2 * DEPTH) ** 0.25
LAM_INIT = 0.8 - 0.6 * math.exp(-0.3 * 0)

T = 512
TM = 512
SKIP_THRESH = 160.0
SKIP_NORM_MARGIN = 1.05
SKIP_ABS_MARGIN = 1.0
FAR_UNROLLS = (8, 4, 2)
ONES_ROWS = 16
VA_ROWS = LANES + ONES_ROWS
VB_ROWS = HALF + ONES_ROWS
VMEM_LIMIT = 60 * 1024 * 1024


def _ada_kernel(c_ref, w_ref, b_ref, q1_ref, k1_ref, q2_ref, k2_ref, ada_ref, lam_ref):
    c = c_ref[...]
    sc = c * jax.nn.sigmoid(c)
    ada_ref[...] = jnp.dot(sc, w_ref[...], precision=lax.Precision.HIGHEST,
                           preferred_element_type=F32) + b_ref[...]
    s1 = jnp.sum(q1_ref[...] * k1_ref[...], axis=-1, keepdims=True)
    s2 = jnp.sum(q2_ref[...] * k2_ref[...], axis=-1, keepdims=True)
    lam = jnp.exp(s1) - jnp.exp(s2) + LAM_INIT
    lam_ref[...] = jnp.broadcast_to(lam, lam_ref.shape)


def _ada_call(c8, w_ada, b_ada, q1, k1, q2, k2):
    d3 = w_ada.shape[1]
    return pl.pallas_call(
        _ada_kernel,
        out_shape=(jax.ShapeDtypeStruct((c8.shape[0], d3), F32),
                   jax.ShapeDtypeStruct((8, LANES), F32)),
        compiler_params=pltpu.CompilerParams(vmem_limit_bytes=VMEM_LIMIT),
        name="ada",
    )(c8, w_ada, b_ada, q1, k1, q2, k2)


def _t5_bucket(rel):
    nb = N_BUCKETS // 2
    ret = jnp.where(rel > 0, nb, 0)
    n = jnp.abs(rel)
    max_exact = nb // 2
    nf = jnp.maximum(n, 1).astype(F32)
    large = max_exact + (jnp.log(nf / max_exact) / math.log(MAX_DISTANCE / max_exact)
                         * (nb - max_exact)).astype(jnp.int32)
    large = jnp.minimum(large, nb - 1)
    return ret + jnp.where(n < max_exact, n, large)


def _table_kernel(rb_ref, tab_ref):
    i = pl.program_id(0)
    h = jnp.minimum(i, A_HEADS - 1)
    kl = lax.broadcasted_iota(jnp.int32, (T, T), 0)
    ql = lax.broadcasted_iota(jnp.int32, (T, T), 1)
    far = rb_ref[N_BUCKETS // 2 - 1, h]
    for d in range(2):
        rel = kl - ql + (d - 1) * T
        bucket = _t5_bucket(rel)
        bias = jnp.zeros((T, T), F32)
        for bk in range(N_BUCKETS):
            bias = jnp.where(bucket == bk, rb_ref[bk, h], bias)
        tab_a = (bias - far) * LOG2E
        if d == 1:
            tab_a = jnp.where((kl // CHUNK) <= (ql // CHUNK), tab_a, NEG_INF)
            tab_b = jnp.where(kl <= ql, 0.0, NEG_INF)
        else:
            tab_b = jnp.zeros((T, T), F32)
        tab = jnp.where(i < A_HEADS, tab_a, tab_b)
        tab_ref[0, d, :, :T] = tab
        tab_ref[0, d, :, T:] = tab


def _table_call(rel_bias):
    return pl.pallas_call(
        _table_kernel,
        grid=(A_HEADS + 1,),
        in_specs=[pl.BlockSpec(memory_space=pltpu.SMEM)],
        out_specs=pl.BlockSpec((1, 2, T, 2 * T), lambda i: (i, 0, 0, 0)),
        out_shape=jax.ShapeDtypeStruct((A_HEADS + 1, 2, T, 2 * T), F32),
        compiler_params=pltpu.CompilerParams(
            dimension_semantics=("arbitrary",), vmem_limit_bytes=VMEM_LIMIT),
        name="tables",
    )(rel_bias)


def _split3(v):
    hi = v.astype(BF16)
    r1 = v - hi.astype(F32)
    mid = r1.astype(BF16)
    lo = (r1 - mid.astype(F32)).astype(BF16)
    return hi, mid, lo


def _proj_kernel(x_ref, sh_ref, sc_ref, wrow_ref, wt_ref, bfb_ref,
                 k_ref, g_ref, qt_ref, vta_ref, vtb_ref, e_ref, kn_ref, nc_ref, carry_ref):
    t = pl.program_id(1)
    d = x_ref.shape[2]
    x = x_ref[0]
    mu = jnp.mean(x, axis=-1, keepdims=True)
    xc = x - mu
    var = jnp.mean(xc * xc, axis=-1, keepdims=True)
    h = (xc * lax.rsqrt(var + LN_EPS)) * (1.0 + sc_ref[0]) + sh_ref[0]
    hb = h.astype(BF16)

    r = jnp.dot(hb, wrow_ref[...], preferred_element_type=F32)
    kb = r[:, :d].astype(BF16)
    k_ref[0] = kb
    g = r[:, d:2 * d]
    g_ref[0] = (g * jax.nn.sigmoid(g)).astype(BF16)

    z = r[:, 2 * d:] + bfb_ref[...]
    lane = lax.broadcasted_iota(jnp.int32, (TM, LANES), 1)
    logf = jnp.minimum(z, 0.0) - jnp.log1p(jnp.exp(-jnp.abs(z)))
    logf = jnp.where(lane < B_HEADS, logf, 0.0)
    ri = lax.broadcasted_iota(jnp.int32, (TM, TM), 0)
    ci = lax.broadcasted_iota(jnp.int32, (TM, TM), 1)
    tri = jnp.where(ci <= ri, 1.0, 0.0).astype(BF16)
    hi, mid, lo = _split3(logf)
    cs = jnp.dot(tri, jnp.concatenate([hi, mid, lo], axis=1), preferred_element_type=F32)

    @pl.when(t == 0)
    def _():
        carry_ref[...] = jnp.zeros_like(carry_ref)

    cf = (cs[:, :LANES] + cs[:, LANES:2 * LANES]) + cs[:, 2 * LANES:] + carry_ref[...]
    carry_ref[...] = cf[TM - 1:TM, :]
    ncf = -cf * LOG2E
    vh, vm, vl = _split3(ncf)

    kf = kb[:, d // 2:].astype(F32)
    gc = lax.broadcasted_iota(jnp.int32, (d // 2, LANES), 0)
    gh = lax.broadcasted_iota(jnp.int32, (d // 2, LANES), 1)
    gsel = jnp.where(gc // HALF == gh, 1.0, 0.0).astype(BF16)
    ksq = jnp.dot((kf * kf).astype(BF16), gsel, preferred_element_type=F32)
    knt = jnp.sqrt(jnp.max(ksq, axis=0, keepdims=True))
    nct = jnp.max(ncf, axis=0, keepdims=True)
    for hh in range(B_HEADS):
        kn_ref[0, hh, 0] = jnp.broadcast_to(knt[:, hh:hh + 1], (8, LANES))
        nc_ref[0, hh, 0] = jnp.broadcast_to(nct[:, hh:hh + 1], (8, LANES))

    e = jnp.where(lane < B_HEADS, vh.astype(F32),
                  jnp.where(lane < 2 * B_HEADS, pltpu.roll(vm.astype(F32), B_HEADS, 1),
                            jnp.where(lane < 3 * B_HEADS,
                                      pltpu.roll(vl.astype(F32), 2 * B_HEADS, 1), 0.0)))
    e_ref[0] = e.astype(BF16)

    tt = lax.dot_general(wt_ref[...], hb, (((1,), (1,)), ((), ())),
                         preferred_element_type=F32)
    qt_ref[0] = (tt[:d] * (LOG2E * HALF ** -0.5)).astype(BF16)
    hd = d // 2
    vta_ref[0, :, 0, :LANES, :] = tt[d:d + hd].reshape(A_HEADS, LANES, TM).astype(BF16)
    vtb_ref[0, :, 0, :HALF, :] = tt[d + hd:].reshape(B_HEADS, HALF, TM).astype(BF16)
    vta_ref[0, :, 0, LANES:, :] = jnp.ones((A_HEADS, ONES_ROWS, TM), BF16)
    vtb_ref[0, :, 0, HALF:, :] = jnp.ones((B_HEADS, ONES_ROWS, TM), BF16)


def _proj_call(x, shift, scale, wrow, wt, bfb):
    b, s, d = x.shape
    nt = s // TM
    nrow = wrow.shape[1]
    return pl.pallas_call(
        _proj_kernel,
        grid=(b, nt),
        in_specs=[
            pl.BlockSpec((1, TM, d), lambda i, t: (i, t, 0)),
            pl.BlockSpec((1, 1, d), lambda i, t: (i, 0, 0)),
            pl.BlockSpec((1, 1, d), lambda i, t: (i, 0, 0)),
            pl.BlockSpec((d, nrow), lambda i, t: (0, 0)),
            pl.BlockSpec((2 * d, d), lambda i, t: (0, 0)),
            pl.BlockSpec((1, LANES), lambda i, t: (0, 0)),
        ],
        out_specs=(
            pl.BlockSpec((1, TM, d), lambda i, t: (i, t, 0)),
            pl.BlockSpec((1, TM, d), lambda i, t: (i, t, 0)),
            pl.BlockSpec((1, d, TM), lambda i, t: (i, 0, t)),
            pl.BlockSpec((1, A_HEADS, 1, VA_ROWS, TM), lambda i, t: (i, 0, t, 0, 0)),
            pl.BlockSpec((1, B_HEADS, 1, VB_ROWS, TM), lambda i, t: (i, 0, t, 0, 0)),
            pl.BlockSpec((1, TM, LANES), lambda i, t: (i, t, 0)),
            pl.BlockSpec((1, B_HEADS, 1, 8, LANES), lambda i, t: (i, 0, t, 0, 0)),
            pl.BlockSpec((1, B_HEADS, 1, 8, LANES), lambda i, t: (i, 0, t, 0, 0)),
        ),
        out_shape=(
            jax.ShapeDtypeStruct((b, s, d), BF16),
            jax.ShapeDtypeStruct((b, s, d), BF16),
            jax.ShapeDtypeStruct((b, d, s), BF16),
            jax.ShapeDtypeStruct((b, A_HEADS, nt, VA_ROWS, TM), BF16),
            jax.ShapeDtypeStruct((b, B_HEADS, nt, VB_ROWS, TM), BF16),
            jax.ShapeDtypeStruct((b, s, LANES), BF16),
            jax.ShapeDtypeStruct((b, B_HEADS, nt, 8, LANES), F32),
            jax.ShapeDtypeStruct((b, B_HEADS, nt, 8, LANES), F32),
        ),
        scratch_shapes=[pltpu.VMEM((1, LANES), F32)],
        compiler_params=pltpu.CompilerParams(
            dimension_semantics=("arbitrary", "arbitrary"), vmem_limit_bytes=VMEM_LIMIT),
        name="proj",
    )(x, shift, scale, wrow, wt, bfb)


def _attn_kernel(*refs, is_a):
    if is_a:
        (qt_ref, qtn_ref, k_ref, vt_ref, g_ref, tab_ref, lam_ref, ng_ref,
         y_ref, qtb_ref, m_ref, acc_ref, s_ref, mt_ref) = refs
        e_ref = None
    else:
        (qt_ref, qtn_ref, k_ref, e_ref, vt_ref, g_ref, tab_ref, kn_ref, nc_ref,
         y_ref, qtb_ref, m_ref, acc_ref, s_ref, mt_ref) = refs
    p = pl.program_id(1)
    qi = pl.program_id(2)
    nq = pl.num_programs(2)
    row = lax.broadcasted_iota(jnp.int32, (LANES, T), 0)

    def halves(ref):
        qt = ref[0].astype(F32)
        return jnp.where(row < HALF, qt, 0.0), jnp.where(row >= HALF, qt, 0.0)

    def set_stationary(ref):
        lo, hi = halves(ref)
        qtb_ref[:LANES, :] = jnp.concatenate([lo, hi], axis=1).astype(BF16)
        if not is_a:
            r2 = lax.broadcasted_iota(jnp.int32, (LANES, 2 * T), 0)
            c2 = lax.broadcasted_iota(jnp.int32, (LANES, 2 * T), 1)
            head = 2 * p + jnp.where(c2 >= T, 1, 0)
            sel = (r2 == head) | (r2 == head + B_HEADS) | (r2 == head + 2 * B_HEADS)
            qtb_ref[LANES:, :] = jnp.where(sel, 1.0, 0.0).astype(BF16)

    def scores(j):
        rows = pl.ds(pl.multiple_of(j * T, T), T)
        kt = k_ref[0, rows, :]
        if not is_a:
            kt = jnp.concatenate([kt, e_ref[0, rows, :]], axis=1)
        return jnp.dot(kt, qtb_ref[...], preferred_element_type=F32)

    def pv_dot(vts, pb):
        if is_a:
            return jnp.dot(vts[0], pb, preferred_element_type=F32)
        return jnp.concatenate(
            [jnp.dot(vts[0], pb[:, :T], preferred_element_type=F32),
             jnp.dot(vts[1], pb[:, T:], preferred_element_type=F32)], axis=1)

    def softmax_pv(vts, s, mt, first=False):
        if first:
            m_new = mt
            acc_ref[...] = pv_dot(vts, jnp.exp2(s - m_new).astype(BF16))
        else:
            m_old = m_ref[...]
            m_new = jnp.maximum(m_old, mt)
            alpha = jnp.exp2(m_old - m_new)
            pv = pv_dot(vts, jnp.exp2(s - m_new).astype(BF16))
            acc_ref[...] = alpha * acc_ref[...] + pv
        m_ref[...] = m_new

    def v_tile(j, valid=None):
        jc = jnp.maximum(j, 0)
        vts = [vt_ref[0, u, jc] for u in range(1 if is_a else 2)]
        if valid is not None:
            vts = [jnp.where(valid, vt, jnp.zeros_like(vt)) for vt in vts]
        return vts

    def scores_to(slot, j):
        s = scores(jnp.maximum(j, 0))
        s_ref[slot] = s
        mt_ref[slot] = jnp.max(s, axis=0, keepdims=True)

    @pl.when(qi == 0)
    def _():
        set_stationary(qt_ref)
        s_ref[1] = scores(qi)

    s_diag = s_ref[1] + tab_ref[0, 1]
    if is_a:
        s_near = scores(jnp.maximum(qi - 1, 0)) + tab_ref[0, 0]
        softmax_pv(v_tile(qi), s_diag, jnp.max(s_diag, axis=0, keepdims=True), first=True)
        top = qi - 2
        scores_to(0, top)
        softmax_pv(v_tile(qi - 1, qi >= 1), s_near, jnp.max(s_near, axis=0, keepdims=True))
        n_far = jnp.maximum(qi - 1, 0)
    else:
        top = qi - 1
        scores_to(0, top)
        softmax_pv(v_tile(qi), s_diag, jnp.max(s_diag, axis=0, keepdims=True), first=True)
        need = None
        for half, qh in enumerate(halves(qt_ref)):
            qn = jnp.sqrt(jnp.max(jnp.sum(qh * qh, axis=0, keepdims=True), axis=1, keepdims=True))
            m_min = jnp.min(m_ref[:, half * T:(half + 1) * T], axis=1, keepdims=True)
            kn = jnp.max(kn_ref[0, half], axis=0)
            bound = (SKIP_NORM_MARGIN * qn) * kn + SKIP_ABS_MARGIN
            nd = (nc_ref[0, half] + bound) > (m_min - SKIP_THRESH)
            need = nd if need is None else (need | nd)
        tiles = need.shape[0]
        jidx = lax.broadcasted_iota(jnp.int32, need.shape, 0)
        cand = jnp.where(need & (jidx < qi), jidx, qi)
        n_far = qi - jnp.min(cand.reshape(tiles * 8, LANES))

    def far_steps(t0, count):
        for u in range(count):
            t = t0 + u
            scores_to((u + 1) % 2, top - t - 1)
            valid = None if u % 2 == 0 else t < n_far
            softmax_pv(v_tile(top - t, valid), s_ref[u % 2], mt_ref[u % 2])

    done = 0
    left = n_far + (n_far & 1)
    for unroll in FAR_UNROLLS:
        trips = left // unroll

        def body(i, carry, unroll=unroll, done=done):
            far_steps(done + unroll * i, unroll)
            return carry

        lax.fori_loop(0, trips, body, 0)
        done = done + unroll * trips
        left = left - unroll * trips

    vd = acc_ref.shape[0] - ONES_ROWS
    o = acc_ref[:vd] * (1.0 / acc_ref[vd:vd + 1])
    if is_a:
        lam = lam_ref[0:1, 0:1]
        o2 = o[:, :T] - lam * o[:, T:]
        ms = jnp.mean(o2 * o2, axis=0, keepdims=True)
        o2 = o2 * lax.rsqrt(ms + LN_EPS) * ng_ref[...] * (1.0 - LAM_INIT)
    else:
        o2 = jnp.concatenate([o[:, :T], o[:, T:]], axis=0)
    y = o2.T * g_ref[0].astype(F32)
    y_ref[0] = y.astype(BF16)

    set_stationary(qtn_ref)
    s_ref[1] = scores(jnp.minimum(qi + 1, nq - 1))


def _attn_call(qt, k, e, vt5, g, tab, lam, ng, kn, nc, *, is_a):
    b, d, s = qt.shape
    nq = s // T
    off = 0 if is_a else PAIRS
    kd = LANES if is_a else 2 * LANES
    v_rows = vt5.shape[3]
    in_specs = [
        pl.BlockSpec((1, LANES, T), lambda i, p, q: (i, p + off, q)),
        pl.BlockSpec((1, LANES, T), lambda i, p, q: (i, p + off, jnp.minimum(q + 1, nq - 1))),
        pl.BlockSpec((1, s, LANES), lambda i, p, q: (i, 0, p + off)),
    ]
    args = [qt, qt, k]
    if not is_a:
        in_specs.append(pl.BlockSpec((1, s, LANES), lambda i, p, q: (i, 0, 0)))
        args.append(e)
    in_specs += [
        pl.BlockSpec((1, 1 if is_a else 2, s // TM, v_rows, TM), lambda i, p, q: (i, p, 0, 0, 0)),
        pl.BlockSpec((1, T, LANES), lambda i, p, q: (i, q, p + off)),
    ]
    args += [vt5, g]
    if is_a:
        in_specs.append(pl.BlockSpec((1, 2, T, 2 * T), lambda i, p, q: (p, 0, 0, 0)))
        in_specs.append(pl.BlockSpec((8, LANES), lambda i, p, q: (0, 0)))
        in_specs.append(pl.BlockSpec((LANES, 1), lambda i, p, q: (0, 0)))
        args += [tab, lam, ng]
    else:
        in_specs.append(pl.BlockSpec((1, 2, T, 2 * T), lambda i, p, q: (A_HEADS, 0, 0, 0)))
        stat_spec = pl.BlockSpec((1, 2, s // TM, 8, LANES), lambda i, p, q: (i, p, 0, 0, 0))
        in_specs += [stat_spec, stat_spec]
        args += [tab, kn, nc]
    return pl.pallas_call(
        functools.partial(_attn_kernel, is_a=is_a),
        grid=(b, PAIRS, nq),
        in_specs=in_specs,
        out_specs=pl.BlockSpec((1, T, LANES), lambda i, p, q: (i, q, p)),
        out_shape=jax.ShapeDtypeStruct((b, s, PAIRS * LANES), BF16),
        scratch_shapes=[
            pltpu.VMEM((kd, 2 * T), BF16),
            pltpu.VMEM((1, 2 * T), F32),
            pltpu.VMEM((v_rows, 2 * T), F32),
            pltpu.VMEM((2, T, 2 * T), F32),
            pltpu.VMEM((2, 1, 2 * T), F32),
        ],
        compiler_params=pltpu.CompilerParams(
            dimension_semantics=("arbitrary", "arbitrary", "arbitrary"),
            vmem_limit_bytes=VMEM_LIMIT),
        name="attn_a" if is_a else "attn_b",
    )(*args)


def _out_kernel(ya_ref, yb_ref, w1_ref, w2_ref, x_ref, gate_ref, lng_ref, lnb_ref, o_ref):
    y = (jnp.dot(ya_ref[0], w1_ref[...], preferred_element_type=F32)
         + jnp.dot(yb_ref[0], w2_ref[...], preferred_element_type=F32))
    z = DEEPNORM_ALPHA * x_ref[0] + gate_ref[0] * y
    mu = jnp.mean(z, axis=-1, keepdims=True)
    zc = z - mu
    var = jnp.mean(zc * zc, axis=-1, keepdims=True)
    o_ref[0] = zc * lax.rsqrt(var + LN_EPS) * lng_ref[...] + lnb_ref[...]


def _out_call(ya, yb, w1, w2, x, gate, lng, lnb):
    b, s, d = x.shape
    half = ya.shape[2]
    return pl.pallas_call(
        _out_kernel,
        grid=(b, s // TM),
        in_specs=[
            pl.BlockSpec((1, TM, half), lambda i, t: (i, t, 0)),
            pl.BlockSpec((1, TM, half), lambda i, t: (i, t, 0)),
            pl.BlockSpec((half, d), lambda i, t: (0, 0)),
            pl.BlockSpec((half, d), lambda i, t: (0, 0)),
            pl.BlockSpec((1, TM, d), lambda i, t: (i, t, 0)),
            pl.BlockSpec((1, 1, d), lambda i, t: (i, 0, 0)),
            pl.BlockSpec((1, d), lambda i, t: (0, 0)),
            pl.BlockSpec((1, d), lambda i, t: (0, 0)),
        ],
        out_specs=pl.BlockSpec((1, TM, d), lambda i, t: (i, t, 0)),
        out_shape=jax.ShapeDtypeStruct((b, s, d), x.dtype),
        compiler_params=pltpu.CompilerParams(
            dimension_semantics=("arbitrary", "arbitrary"), vmem_limit_bytes=VMEM_LIMIT),
        name="out",
    )(ya, yb, w1, w2, x, gate, lng, lnb)


def kernel(x, c, w_in, w_out, rel_bias, lam_q1, lam_k1, lam_q2, lam_k2, diff_norm_g,
           b_forget, w_ada, b_ada, ln_g, ln_b):
    b, s, d = x.shape
    assert s % T == 0 and T == TM and d == PAIRS * 2 * LANES
    layer = 0
    aw = PAIRS * LANES

    c8 = jnp.pad(c, ((0, 8 - b), (0, 0)))
    ada8, lam = _ada_call(c8, w_ada[layer], b_ada[layer][None], lam_q1[layer][None],
                          lam_k1[layer][None], lam_q2[layer][None], lam_k2[layer][None])
    ada = ada8[:b]
    shift = ada[:, None, :d]
    scale = ada[:, None, d:2 * d]
    gate = ada[:, None, 2 * d:]

    w = w_in[layer]
    grp = lambda i: w[:, i * aw:(i + 1) * aw]
    wbf = jnp.pad(w[:, 8 * aw:], ((0, 0), (0, LANES - B_HEADS)))
    wrow = jnp.concatenate([grp(1), grp(5), grp(3), grp(7), wbf], axis=1).astype(BF16)
    wt = jnp.concatenate([grp(0), grp(4), grp(2), grp(6)], axis=1).T.astype(BF16)
    bfb = jnp.pad(b_forget[layer], (0, LANES - B_HEADS))[None].astype(F32)

    k, g, qt, vta, vtb, e, kn, nc = _proj_call(x, shift, scale, wrow, wt, bfb)
    tab = _table_call(rel_bias)
    ng = diff_norm_g[layer].reshape(LANES, 1)
    ya = _attn_call(qt, k, None, vta, g, tab, lam, ng, None, None, is_a=True)
    yb = _attn_call(qt, k, e, vtb, g, tab, None, None, kn, nc, is_a=False)

    wo = w_out[layer].astype(BF16)
    return _out_call(ya, yb, wo[:aw], wo[aw:], x, gate, ln_g[layer][None], ln_b[layer][None])
```

```python
import functools
import math

import jax
import jax.numpy as jnp
from jax import lax
from jax.experimental import pallas as pl
from jax.experimental.pallas import tpu as pltpu

F32 = jnp.float32
BF16 = jnp.bfloat16

LANES = 128
HALF = 64
A_HEADS = 4
B_HEADS = 8
PAIRS = 4
CHUNK = 64
N_BUCKETS = 32
MAX_DISTANCE = 128
LN_EPS = 1e-5
NEG_INF = -1e30
LOG2E = 1.4426950408889634
DEPTH = 1
DEEPNORM_ALPHA = (2 * DEPTH) ** 0.25
LAM_INIT = 0.8 - 0.6 * math.exp(-0.3 * 0)

T = 512
BLK = 128
TM = 512
SKIP_THRESH = 160.0
SKIP_NORM_MARGIN = 1.05
SKIP_ABS_MARGIN = 1.0
FAR_UNROLLS = (8, 4, 2)
ONES_ROWS = 16
VA_ROWS = LANES + ONES_ROWS
VB_ROWS = HALF + ONES_ROWS
VMEM_LIMIT = 60 * 1024 * 1024


def _ada_kernel(c_ref, w_ref, b_ref, q1_ref, k1_ref, q2_ref, k2_ref, ada_ref, lam_ref):
    c = c_ref[...]
    sc = c * jax.nn.sigmoid(c)
    ada_ref[...] = jnp.dot(sc, w_ref[...], precision=lax.Precision.HIGHEST,
                           preferred_element_type=F32) + b_ref[...]
    s1 = jnp.sum(q1_ref[...] * k1_ref[...], axis=-1, keepdims=True)
    s2 = jnp.sum(q2_ref[...] * k2_ref[...], axis=-1, keepdims=True)
    lam = jnp.exp(s1) - jnp.exp(s2) + LAM_INIT
    lam_ref[...] = jnp.broadcast_to(lam, lam_ref.shape)


def _ada_call(c8, w_ada, b_ada, q1, k1, q2, k2):
    d3 = w_ada.shape[1]
    return pl.pallas_call(
        _ada_kernel,
        out_shape=(jax.ShapeDtypeStruct((c8.shape[0], d3), F32),
                   jax.ShapeDtypeStruct((8, LANES), F32)),
        compiler_params=pltpu.CompilerParams(vmem_limit_bytes=VMEM_LIMIT),
        name="ada",
    )(c8, w_ada, b_ada, q1, k1, q2, k2)


def _t5_bucket(rel):
    nb = N_BUCKETS // 2
    ret = jnp.where(rel > 0, nb, 0)
    n = jnp.abs(rel)
    max_exact = nb // 2
    nf = jnp.maximum(n, 1).astype(F32)
    large = max_exact + (jnp.log(nf / max_exact) / math.log(MAX_DISTANCE / max_exact)
                         * (nb - max_exact)).astype(jnp.int32)
    large = jnp.minimum(large, nb - 1)
    return ret + jnp.where(n < max_exact, n, large)


def _table_kernel(rb_ref, tab_ref):
    i = pl.program_id(0)
    h = jnp.minimum(i, A_HEADS - 1)
    kl = lax.broadcasted_iota(jnp.int32, (BLK, BLK), 0)
    ql = lax.broadcasted_iota(jnp.int32, (BLK, BLK), 1)
    far = rb_ref[N_BUCKETS // 2 - 1, h]
    for d in range(2):
        bucket = _t5_bucket(kl - ql + (d - 1) * BLK)
        bias = jnp.zeros((BLK, BLK), F32)
        for bk in range(N_BUCKETS):
            bias = jnp.where(bucket == bk, rb_ref[bk, h], bias)
        tab_a = (bias - far) * LOG2E
        if d == 1:
            tab_a = jnp.where((kl // CHUNK) <= (ql // CHUNK), tab_a, NEG_INF)
            tab_b = jnp.where(kl <= ql, 0.0, NEG_INF)
        else:
            tab_b = jnp.zeros((BLK, BLK), F32)
        tab_ref[0, d] = jnp.where(i < A_HEADS, tab_a, tab_b)


def _table_call(rel_bias):
    return pl.pallas_call(
        _table_kernel,
        grid=(A_HEADS + 1,),
        in_specs=[pl.BlockSpec(memory_space=pltpu.SMEM)],
        out_specs=pl.BlockSpec((1, 2, BLK, BLK), lambda i: (i, 0, 0, 0)),
        out_shape=jax.ShapeDtypeStruct((A_HEADS + 1, 2, BLK, BLK), F32),
        compiler_params=pltpu.CompilerParams(
            dimension_semantics=("arbitrary",), vmem_limit_bytes=VMEM_LIMIT),
        name="tables",
    )(rel_bias)


def _split3(v):
    hi = v.astype(BF16)
    r1 = v - hi.astype(F32)
    mid = r1.astype(BF16)
    lo = (r1 - mid.astype(F32)).astype(BF16)
    return hi, mid, lo


def _proj_kernel(x_ref, sh_ref, sc_ref, wrow_ref, wt_ref, bfb_ref,
                 k_ref, g_ref, qt_ref, vta_ref, vtb_ref, e_ref, kn_ref, nc_ref, carry_ref):
    t = pl.program_id(1)
    d = x_ref.shape[2]
    x = x_ref[0]
    mu = jnp.mean(x, axis=-1, keepdims=True)
    xc = x - mu
    var = jnp.mean(xc * xc, axis=-1, keepdims=True)
    h = (xc * lax.rsqrt(var + LN_EPS)) * (1.0 + sc_ref[0]) + sh_ref[0]
    hb = h.astype(BF16)

    r = jnp.dot(hb, wrow_ref[...], preferred_element_type=F32)
    kb = r[:, :d].astype(BF16)
    k_ref[0] = kb
    g = r[:, d:2 * d]
    g_ref[0] = (g * jax.nn.sigmoid(g)).astype(BF16)

    z = r[:, 2 * d:] + bfb_ref[...]
    lane = lax.broadcasted_iota(jnp.int32, (TM, LANES), 1)
    logf = jnp.minimum(z, 0.0) - jnp.log1p(jnp.exp(-jnp.abs(z)))
    logf = jnp.where(lane < B_HEADS, logf, 0.0)
    ri = lax.broadcasted_iota(jnp.int32, (TM, TM), 0)
    ci = lax.broadcasted_iota(jnp.int32, (TM, TM), 1)
    tri = jnp.where(ci <= ri, 1.0, 0.0).astype(BF16)
    hi, mid, lo = _split3(logf)
    cs = jnp.dot(tri, jnp.concatenate([hi, mid, lo], axis=1), preferred_element_type=F32)

    @pl.when(t == 0)
    def _():
        carry_ref[...] = jnp.zeros_like(carry_ref)

    cf = (cs[:, :LANES] + cs[:, LANES:2 * LANES]) + cs[:, 2 * LANES:] + carry_ref[...]
    carry_ref[...] = cf[TM - 1:TM, :]
    ncf = -cf * LOG2E
    vh, vm, vl = _split3(ncf)

    kf = kb[:, d // 2:].astype(F32)
    gc = lax.broadcasted_iota(jnp.int32, (d // 2, LANES), 0)
    gh = lax.broadcasted_iota(jnp.int32, (d // 2, LANES), 1)
    gsel = jnp.where(gc // HALF == gh, 1.0, 0.0).astype(BF16)
    ksq = jnp.dot((kf * kf).astype(BF16), gsel, preferred_element_type=F32)
    knt = jnp.sqrt(jnp.max(ksq, axis=0, keepdims=True))
    nct = jnp.max(ncf, axis=0, keepdims=True)
    for hh in range(B_HEADS):
        kn_ref[0, hh, 0] = jnp.broadcast_to(knt[:, hh:hh + 1], (8, LANES))
        nc_ref[0, hh, 0] = jnp.broadcast_to(nct[:, hh:hh + 1], (8, LANES))

    e = jnp.where(lane < B_HEADS, vh.astype(F32),
                  jnp.where(lane < 2 * B_HEADS, pltpu.roll(vm.astype(F32), B_HEADS, 1),
                            jnp.where(lane < 3 * B_HEADS,
                                      pltpu.roll(vl.astype(F32), 2 * B_HEADS, 1), 0.0)))
    e_ref[0] = e.astype(BF16)

    tt = lax.dot_general(wt_ref[...], hb, (((1,), (1,)), ((), ())),
                         preferred_element_type=F32)
    qt_ref[0] = (tt[:d] * (LOG2E * HALF ** -0.5)).astype(BF16)
    hd = d // 2
    vta_ref[0, :, 0, :LANES, :] = tt[d:d + hd].reshape(A_HEADS, LANES, TM).astype(BF16)
    vtb_ref[0, :, 0, :HALF, :] = tt[d + hd:].reshape(B_HEADS, HALF, TM).astype(BF16)
    vta_ref[0, :, 0, LANES:, :] = jnp.ones((A_HEADS, ONES_ROWS, TM), BF16)
    vtb_ref[0, :, 0, HALF:, :] = jnp.ones((B_HEADS, ONES_ROWS, TM), BF16)


def _proj_call(x, shift, scale, wrow, wt, bfb):
    b, s, d = x.shape
    nt = s // TM
    nrow = wrow.shape[1]
    return pl.pallas_call(
        _proj_kernel,
        grid=(b, nt),
        in_specs=[
            pl.BlockSpec((1, TM, d), lambda i, t: (i, t, 0)),
            pl.BlockSpec((1, 1, d), lambda i, t: (i, 0, 0)),
            pl.BlockSpec((1, 1, d), lambda i, t: (i, 0, 0)),
            pl.BlockSpec((d, nrow), lambda i, t: (0, 0)),
            pl.BlockSpec((2 * d, d), lambda i, t: (0, 0)),
            pl.BlockSpec((1, LANES), lambda i, t: (0, 0)),
        ],
        out_specs=(
            pl.BlockSpec((1, TM, d), lambda i, t: (i, t, 0)),
            pl.BlockSpec((1, TM, d), lambda i, t: (i, t, 0)),
            pl.BlockSpec((1, d, TM), lambda i, t: (i, 0, t)),
            pl.BlockSpec((1, A_HEADS, 1, VA_ROWS, TM), lambda i, t: (i, 0, t, 0, 0)),
            pl.BlockSpec((1, B_HEADS, 1, VB_ROWS, TM), lambda i, t: (i, 0, t, 0, 0)),
            pl.BlockSpec((1, TM, LANES), lambda i, t: (i, t, 0)),
            pl.BlockSpec((1, B_HEADS, 1, 8, LANES), lambda i, t: (i, 0, t, 0, 0)),
            pl.BlockSpec((1, B_HEADS, 1, 8, LANES), lambda i, t: (i, 0, t, 0, 0)),
        ),
        out_shape=(
            jax.ShapeDtypeStruct((b, s, d), BF16),
            jax.ShapeDtypeStruct((b, s, d), BF16),
            jax.ShapeDtypeStruct((b, d, s), BF16),
            jax.ShapeDtypeStruct((b, A_HEADS, nt, VA_ROWS, TM), BF16),
            jax.ShapeDtypeStruct((b, B_HEADS, nt, VB_ROWS, TM), BF16),
            jax.ShapeDtypeStruct((b, s, LANES), BF16),
            jax.ShapeDtypeStruct((b, B_HEADS, nt, 8, LANES), F32),
            jax.ShapeDtypeStruct((b, B_HEADS, nt, 8, LANES), F32),
        ),
        scratch_shapes=[pltpu.VMEM((1, LANES), F32)],
        compiler_params=pltpu.CompilerParams(
            dimension_semantics=("arbitrary", "arbitrary"), vmem_limit_bytes=VMEM_LIMIT),
        name="proj",
    )(x, shift, scale, wrow, wt, bfb)


def _attn_kernel(*refs, is_a):
    if is_a:
        (qt_ref, k_ref, vt_ref, g_ref, tab_ref, lam_ref, ng_ref,
         y_ref, qtb_ref, m_ref, acc_ref, s_ref, mt_ref) = refs
        e_ref = None
    else:
        (qt_ref, k_ref, e_ref, vt_ref, g_ref, tab_ref, kn_ref, nc_ref,
         y_ref, qtb_ref, m_ref, acc_ref, s_ref, mt_ref) = refs
    p = pl.program_id(1)
    qi = pl.program_id(2)

    qt = qt_ref[0].astype(F32)
    row = lax.broadcasted_iota(jnp.int32, (LANES, T), 0)
    q_lo = jnp.where(row < HALF, qt, 0.0)
    q_hi = jnp.where(row >= HALF, qt, 0.0)
    qtb_ref[:LANES, :] = jnp.concatenate([q_lo, q_hi], axis=1).astype(BF16)
    if not is_a:
        r2 = lax.broadcasted_iota(jnp.int32, (LANES, 2 * T), 0)
        c2 = lax.broadcasted_iota(jnp.int32, (LANES, 2 * T), 1)
        head = 2 * p + jnp.where(c2 >= T, 1, 0)
        sel = (r2 == head) | (r2 == head + B_HEADS) | (r2 == head + 2 * B_HEADS)
        qtb_ref[LANES:, :] = jnp.where(sel, 1.0, 0.0).astype(BF16)

    def scores(j):
        rows = pl.ds(pl.multiple_of(j * T, T), T)
        kt = k_ref[0, rows, :]
        if not is_a:
            kt = jnp.concatenate([kt, e_ref[0, rows, :]], axis=1)
        return jnp.dot(kt, qtb_ref[...], preferred_element_type=F32)

    def pv_dot(vts, pb):
        if is_a:
            return jnp.dot(vts[0], pb, preferred_element_type=F32)
        return jnp.concatenate(
            [jnp.dot(vts[0], pb[:, :T], preferred_element_type=F32),
             jnp.dot(vts[1], pb[:, T:], preferred_element_type=F32)], axis=1)

    def softmax_pv(vts, s, mt):
        m_old = m_ref[...]
        m_new = jnp.maximum(m_old, mt)
        alpha = jnp.exp2(m_old - m_new)
        pv = pv_dot(vts, jnp.exp2(s - m_new).astype(BF16))
        acc_ref[...] = alpha * acc_ref[...] + pv
        m_ref[...] = m_new

    def v_tile(j, valid=None):
        jc = jnp.maximum(j, 0)
        vts = [vt_ref[0, u, jc] for u in range(1 if is_a else 2)]
        if valid is not None:
            vts = [jnp.where(valid, vt, jnp.zeros_like(vt)) for vt in vts]
        return vts

    def scores_to(slot, j):
        s = scores(jnp.maximum(j, 0))
        s_ref[slot] = s
        mt_ref[slot] = jnp.max(s, axis=0, keepdims=True)

    nblk = T // BLK
    tab_sub = tab_ref[0, 0] if is_a else None
    tab_diag = tab_ref[0, 1]

    def diag_tile(s):
        pbs, mts = [], []
        for lb in range(2 * nblk):
            qb = lb % nblk
            lanes = slice(lb * BLK, (lb + 1) * BLK)
            parts = [s[:(qb - 1) * BLK, lanes]] if qb >= 2 else []
            if qb >= 1:
                below = s[(qb - 1) * BLK:qb * BLK, lanes]
                parts.append(below if tab_sub is None else below + tab_sub)
            parts.append(s[qb * BLK:(qb + 1) * BLK, lanes] + tab_diag)
            sb = parts[0] if len(parts) == 1 else jnp.concatenate(parts, axis=0)
            mt = jnp.max(sb, axis=0, keepdims=True)
            pb = jnp.exp2(sb - mt).astype(BF16)
            if qb < nblk - 1:
                pb = jnp.concatenate([pb, jnp.zeros((T - (qb + 1) * BLK, BLK), BF16)], axis=0)
            pbs.append(pb)
            mts.append(mt)
        acc_ref[...] = pv_dot(v_tile(qi), jnp.concatenate(pbs, axis=1))
        m_ref[...] = jnp.concatenate(mts, axis=1)

    s_diag = scores(qi)
    if is_a:
        s_near = scores(jnp.maximum(qi - 1, 0))
        last = s_near[T - BLK:]
        last = jnp.concatenate(
            [last[:, :BLK] + tab_sub, last[:, BLK:T], last[:, T:T + BLK] + tab_sub,
             last[:, T + BLK:]], axis=1)
        s_near = jnp.concatenate([s_near[:T - BLK], last], axis=0)
        diag_tile(s_diag)
        top = qi - 2
        scores_to(0, top)
        softmax_pv(v_tile(qi - 1, qi >= 1), s_near, jnp.max(s_near, axis=0, keepdims=True))
        n_far = jnp.maximum(qi - 1, 0)
    else:
        top = qi - 1
        scores_to(0, top)
        diag_tile(s_diag)
        need = None
        for half, qh in enumerate((q_lo, q_hi)):
            qn = jnp.sqrt(jnp.max(jnp.sum(qh * qh, axis=0, keepdims=True), axis=1, keepdims=True))
            m_min = jnp.min(m_ref[:, half * T:(half + 1) * T], axis=1, keepdims=True)
            kn = jnp.max(kn_ref[0, half], axis=0)
            bound = (SKIP_NORM_MARGIN * qn) * kn + SKIP_ABS_MARGIN
            nd = (nc_ref[0, half] + bound) > (m_min - SKIP_THRESH)
            need = nd if need is None else (need | nd)
        tiles = need.shape[0]
        jidx = lax.broadcasted_iota(jnp.int32, need.shape, 0)
        cand = jnp.where(need & (jidx < qi), jidx, qi)
        n_far = qi - jnp.min(cand.reshape(tiles * 8, LANES))

    def far_steps(t0, count):
        for u in range(count):
            t = t0 + u
            scores_to((u + 1) % 2, top - t - 1)
            valid = None if u % 2 == 0 else t < n_far
            softmax_pv(v_tile(top - t, valid), s_ref[u % 2], mt_ref[u % 2])

    done = 0
    left = n_far + (n_far & 1)
    for unroll in FAR_UNROLLS:
        trips = left // unroll

        def body(i, carry, unroll=unroll, done=done):
            far_steps(done + unroll * i, unroll)
            return carry

        lax.fori_loop(0, trips, body, 0)
        done = done + unroll * trips
        left = left - unroll * trips

    vd = acc_ref.shape[0] - ONES_ROWS
    o = acc_ref[:vd] * (1.0 / acc_ref[vd:vd + 1])
    if is_a:
        lam = lam_ref[0:1, 0:1]
        o2 = o[:, :T] - lam * o[:, T:]
        ms = jnp.mean(o2 * o2, axis=0, keepdims=True)
        o2 = o2 * lax.rsqrt(ms + LN_EPS) * ng_ref[...] * (1.0 - LAM_INIT)
    else:
        o2 = jnp.concatenate([o[:, :T], o[:, T:]], axis=0)
    y = o2.T * g_ref[0].astype(F32)
    y_ref[0] = y.astype(BF16)


def _attn_call(qt, k, e, vt5, g, tab, lam, ng, kn, nc, *, is_a):
    b, d, s = qt.shape
    nq = s // T
    off = 0 if is_a else PAIRS
    kd = LANES if is_a else 2 * LANES
    v_rows = vt5.shape[3]
    in_specs = [
        pl.BlockSpec((1, LANES, T), lambda i, p, q: (i, p + off, q)),
        pl.BlockSpec((1, s, LANES), lambda i, p, q: (i, 0, p + off)),
    ]
    args = [qt, k]
    if not is_a:
        in_specs.append(pl.BlockSpec((1, s, LANES), lambda i, p, q: (i, 0, 0)))
        args.append(e)
    in_specs += [
        pl.BlockSpec((1, 1 if is_a else 2, s // TM, v_rows, TM), lambda i, p, q: (i, p, 0, 0, 0)),
        pl.BlockSpec((1, T, LANES), lambda i, p, q: (i, q, p + off)),
    ]
    args += [vt5, g]
    if is_a:
        in_specs.append(pl.BlockSpec((1, 2, BLK, BLK), lambda i, p, q: (p, 0, 0, 0)))
        in_specs.append(pl.BlockSpec((8, LANES), lambda i, p, q: (0, 0)))
        in_specs.append(pl.BlockSpec((LANES, 1), lambda i, p, q: (0, 0)))
        args += [tab, lam, ng]
    else:
        in_specs.append(pl.BlockSpec((1, 2, BLK, BLK), lambda i, p, q: (A_HEADS, 0, 0, 0)))
        stat_spec = pl.BlockSpec((1, 2, s // TM, 8, LANES), lambda i, p, q: (i, p, 0, 0, 0))
        in_specs += [stat_spec, stat_spec]
        args += [tab, kn, nc]
    return pl.pallas_call(
        functools.partial(_attn_kernel, is_a=is_a),
        grid=(b, PAIRS, nq),
        in_specs=in_specs,
        out_specs=pl.BlockSpec((1, T, LANES), lambda i, p, q: (i, q, p)),
        out_shape=jax.ShapeDtypeStruct((b, s, PAIRS * LANES), BF16),
        scratch_shapes=[
            pltpu.VMEM((kd, 2 * T), BF16),
            pltpu.VMEM((1, 2 * T), F32),
            pltpu.VMEM((v_rows, 2 * T), F32),
            pltpu.VMEM((2, T, 2 * T), F32),
            pltpu.VMEM((2, 1, 2 * T), F32),
        ],
        compiler_params=pltpu.CompilerParams(
            dimension_semantics=("arbitrary", "arbitrary", "arbitrary"),
            vmem_limit_bytes=VMEM_LIMIT),
        name="attn_a" if is_a else "attn_b",
    )(*args)


def _out_kernel(ya_ref, yb_ref, w1_ref, w2_ref, x_ref, gate_ref, lng_ref, lnb_ref, o_ref):
    y = (jnp.dot(ya_ref[0], w1_ref[...], preferred_element_type=F32)
         + jnp.dot(yb_ref[0], w2_ref[...], preferred_element_type=F32))
    z = DEEPNORM_ALPHA * x_ref[0] + gate_ref[0] * y
    mu = jnp.mean(z, axis=-1, keepdims=True)
    zc = z - mu
    var = jnp.mean(zc * zc, axis=-1, keepdims=True)
    o_ref[0] = zc * lax.rsqrt(var + LN_EPS) * lng_ref[...] + lnb_ref[...]


def _out_call(ya, yb, w1, w2, x, gate, lng, lnb):
    b, s, d = x.shape
    half = ya.shape[2]
    return pl.pallas_call(
        _out_kernel,
        grid=(b, s // TM),
        in_specs=[
            pl.BlockSpec((1, TM, half), lambda i, t: (i, t, 0)),
            pl.BlockSpec((1, TM, half), lambda i, t: (i, t, 0)),
            pl.BlockSpec((half, d), lambda i, t: (0, 0)),
            pl.BlockSpec((half, d), lambda i, t: (0, 0)),
            pl.BlockSpec((1, TM, d), lambda i, t: (i, t, 0)),
            pl.BlockSpec((1, 1, d), lambda i, t: (i, 0, 0)),
            pl.BlockSpec((1, d), lambda i, t: (0, 0)),
            pl.BlockSpec((1, d), lambda i, t: (0, 0)),
        ],
        out_specs=pl.BlockSpec((1, TM, d), lambda i, t: (i, t, 0)),
        out_shape=jax.ShapeDtypeStruct((b, s, d), x.dtype),
        compiler_params=pltpu.CompilerParams(
            dimension_semantics=("arbitrary", "arbitrary"), vmem_limit_bytes=VMEM_LIMIT),
        name="out",
    )(ya, yb, w1, w2, x, gate, lng, lnb)


def kernel(x, c, w_in, w_out, rel_bias, lam_q1, lam_k1, lam_q2, lam_k2, diff_norm_g,
           b_forget, w_ada, b_ada, ln_g, ln_b):
    b, s, d = x.shape
    assert s % T == 0 and T == TM and d == PAIRS * 2 * LANES
    layer = 0
    aw = PAIRS * LANES

    c8 = jnp.pad(c, ((0, 8 - b), (0, 0)))
    ada8, lam = _ada_call(c8, w_ada[layer], b_ada[layer][None], lam_q1[layer][None],
                          lam_k1[layer][None], lam_q2[layer][None], lam_k2[layer][None])
    ada = ada8[:b]
    shift = ada[:, None, :d]
    scale = ada[:, None, d:2 * d]
    gate = ada[:, None, 2 * d:]

    w = w_in[layer]
    grp = lambda i: w[:, i * aw:(i + 1) * aw]
    wbf = jnp.pad(w[:, 8 * aw:], ((0, 0), (0, LANES - B_HEADS)))
    wrow = jnp.concatenate([grp(1), grp(5), grp(3), grp(7), wbf], axis=1).astype(BF16)
    wt = jnp.concatenate([grp(0), grp(4), grp(2), grp(6)], axis=1).T.astype(BF16)
    bfb = jnp.pad(b_forget[layer], (0, LANES - B_HEADS))[None].astype(F32)

    k, g, qt, vta, vtb, e, kn, nc = _proj_call(x, shift, scale, wrow, wt, bfb)
    tab = _table_call(rel_bias)
    ng = diff_norm_g[layer].reshape(LANES, 1)
    ya = _attn_call(qt, k, None, vta, g, tab, lam, ng, None, None, is_a=True)
    yb = _attn_call(qt, k, e, vtb, g, tab, None, None, kn, nc, is_a=False)

    wo = w_out[layer].astype(BF16)
    return _out_call(ya, yb, wo[:aw], wo[aw:], x, gate, ln_g[layer][None], ln_b[layer][None])
```

```python
import functools
import math

import jax
import jax.numpy as jnp
from jax import lax
from jax.experimental import pallas as pl
from jax.experimental.pallas import tpu as pltpu

F32 = jnp.float32
BF16 = jnp.bfloat16

LANES = 128
HALF = 64
A_HEADS = 4
B_HEADS = 8
PAIRS = 4
CHUNK = 64
N_BUCKETS = 32
MAX_DISTANCE = 128
LN_EPS = 1e-5
NEG_INF = -1e30
LOG2E = 1.4426950408889634
DEPTH = 1
DEEPNORM_ALPHA = (2 * DEPTH) ** 0.25
LAM_INIT = 0.8 - 0.6 * math.exp(-0.3 * 0)

T = 512
BLK = 128
TM = 512
TO = 1024
SKIP_THRESH = 152.0
SKIP_NORM_MARGIN = 1.05
SKIP_ABS_MARGIN = 1.0
FAR_UNROLLS = (8, 4, 2)
ONES_ROWS = 16
VA_ROWS = LANES + ONES_ROWS
VB_ROWS = HALF + ONES_ROWS
VMEM_LIMIT = 60 * 1024 * 1024


def _ada_kernel(c_ref, w_ref, b_ref, q1_ref, k1_ref, q2_ref, k2_ref, ada_ref, lam_ref):
    c = c_ref[...]
    sc = c * jax.nn.sigmoid(c)
    ada_ref[...] = jnp.dot(sc, w_ref[...], precision=lax.Precision.HIGHEST,
                           preferred_element_type=F32) + b_ref[...]
    s1 = jnp.sum(q1_ref[...] * k1_ref[...], axis=-1, keepdims=True)
    s2 = jnp.sum(q2_ref[...] * k2_ref[...], axis=-1, keepdims=True)
    lam = jnp.exp(s1) - jnp.exp(s2) + LAM_INIT
    lam_ref[...] = jnp.broadcast_to(lam, lam_ref.shape)


def _ada_call(c8, w_ada, b_ada, q1, k1, q2, k2):
    d3 = w_ada.shape[1]
    return pl.pallas_call(
        _ada_kernel,
        out_shape=(jax.ShapeDtypeStruct((c8.shape[0], d3), F32),
                   jax.ShapeDtypeStruct((8, LANES), F32)),
        compiler_params=pltpu.CompilerParams(vmem_limit_bytes=VMEM_LIMIT),
        name="ada",
    )(c8, w_ada, b_ada, q1, k1, q2, k2)


def _t5_bucket(rel):
    nb = N_BUCKETS // 2
    ret = jnp.where(rel > 0, nb, 0)
    n = jnp.abs(rel)
    max_exact = nb // 2
    nf = jnp.maximum(n, 1).astype(F32)
    large = max_exact + (jnp.log(nf / max_exact) / math.log(MAX_DISTANCE / max_exact)
                         * (nb - max_exact)).astype(jnp.int32)
    large = jnp.minimum(large, nb - 1)
    return ret + jnp.where(n < max_exact, n, large)


def _table_kernel(rb_ref, tab_ref):
    i = pl.program_id(0)
    h = jnp.minimum(i, A_HEADS - 1)
    kl = lax.broadcasted_iota(jnp.int32, (BLK, BLK), 0)
    ql = lax.broadcasted_iota(jnp.int32, (BLK, BLK), 1)
    far = rb_ref[N_BUCKETS // 2 - 1, h]
    for d in range(2):
        bucket = _t5_bucket(kl - ql + (d - 1) * BLK)
        bias = jnp.zeros((BLK, BLK), F32)
        for bk in range(N_BUCKETS):
            bias = jnp.where(bucket == bk, rb_ref[bk, h], bias)
        tab_a = (bias - far) * LOG2E
        if d == 1:
            tab_a = jnp.where((kl // CHUNK) <= (ql // CHUNK), tab_a, NEG_INF)
            tab_b = jnp.where(kl <= ql, 0.0, NEG_INF)
        else:
            tab_b = jnp.zeros((BLK, BLK), F32)
        tab_ref[0, d] = jnp.where(i < A_HEADS, tab_a, tab_b)


def _table_call(rel_bias):
    return pl.pallas_call(
        _table_kernel,
        grid=(A_HEADS + 1,),
        in_specs=[pl.BlockSpec(memory_space=pltpu.SMEM)],
        out_specs=pl.BlockSpec((1, 2, BLK, BLK), lambda i: (i, 0, 0, 0)),
        out_shape=jax.ShapeDtypeStruct((A_HEADS + 1, 2, BLK, BLK), F32),
        compiler_params=pltpu.CompilerParams(
            dimension_semantics=("arbitrary",), vmem_limit_bytes=VMEM_LIMIT),
        name="tables",
    )(rel_bias)


def _split3(v):
    hi = v.astype(BF16)
    r1 = v - hi.astype(F32)
    mid = r1.astype(BF16)
    lo = (r1 - mid.astype(F32)).astype(BF16)
    return hi, mid, lo


def _proj_kernel(x_ref, sh_ref, sc_ref, wrow_ref, wt_ref, bfb_ref,
                 k_ref, g_ref, qt_ref, vta_ref, vtb_ref, e_ref, kn_ref, nc_ref, carry_ref):
    t = pl.program_id(1)
    d = x_ref.shape[2]
    x = x_ref[0]
    mu = jnp.mean(x, axis=-1, keepdims=True)
    xc = x - mu
    var = jnp.mean(xc * xc, axis=-1, keepdims=True)
    h = (xc * lax.rsqrt(var + LN_EPS)) * (1.0 + sc_ref[0]) + sh_ref[0]
    hb = h.astype(BF16)

    r = jnp.dot(hb, wrow_ref[...], preferred_element_type=F32)
    kb = r[:, :d].astype(BF16)
    k_ref[0] = kb
    g = r[:, d:]
    g_ref[0] = (g * jax.nn.sigmoid(g)).astype(BF16)

    tt = lax.dot_general(wt_ref[...], hb, (((1,), (1,)), ((), ())),
                         preferred_element_type=F32)
    qt_ref[0] = (tt[:d] * (LOG2E * HALF ** -0.5)).astype(BF16)

    z = tt[2 * d:2 * d + B_HEADS] + bfb_ref[...]
    logf = jnp.minimum(z, 0.0) - jnp.log1p(jnp.exp(-jnp.abs(z)))
    ri = lax.broadcasted_iota(jnp.int32, (TM, TM), 0)
    ci = lax.broadcasted_iota(jnp.int32, (TM, TM), 1)
    upper = jnp.where(ri <= ci, 1.0, 0.0).astype(BF16)
    hi, mid, lo = _split3(logf)
    pieces = jnp.concatenate([hi.astype(F32), mid.astype(F32), lo.astype(F32),
                              jnp.zeros((B_HEADS, TM), F32)], axis=0)
    cs = jnp.dot(pieces.astype(BF16), upper, preferred_element_type=F32)

    @pl.when(t == 0)
    def _():
        carry_ref[...] = jnp.zeros_like(carry_ref)

    carry = jnp.concatenate([carry_ref[...]] * (TM // LANES), axis=1)
    cf = (cs[:B_HEADS] + cs[B_HEADS:2 * B_HEADS]) + cs[2 * B_HEADS:3 * B_HEADS] + carry
    carry_ref[...] = jnp.broadcast_to(cf[:, TM - 1:TM], carry_ref.shape)
    ncf = -cf * LOG2E
    vh, vm, vl = _split3(ncf)
    pt = jnp.concatenate([vh.astype(F32), vm.astype(F32), vl.astype(F32),
                          jnp.zeros((LANES - 3 * B_HEADS, TM), F32)], axis=0)
    e_ref[0] = pt.T.astype(BF16)

    kf = kb[:, d // 2:].astype(F32)
    gc = lax.broadcasted_iota(jnp.int32, (d // 2, LANES), 0)
    gh = lax.broadcasted_iota(jnp.int32, (d // 2, LANES), 1)
    gsel = jnp.where(gc // HALF == gh, 1.0, 0.0).astype(BF16)
    ksq = jnp.dot((kf * kf).astype(BF16), gsel, preferred_element_type=F32)
    knt = jnp.sqrt(jnp.max(ksq, axis=0, keepdims=True))
    nct = jnp.max(ncf, axis=1, keepdims=True)
    for hh in range(B_HEADS):
        kn_ref[0, hh, 0] = jnp.broadcast_to(knt[:, hh:hh + 1], (8, LANES))
        nc_ref[0, hh, 0] = jnp.broadcast_to(nct[hh:hh + 1, :], (8, LANES))

    hd = d // 2
    vta_ref[0, :, 0, :LANES, :] = tt[d:d + hd].reshape(A_HEADS, LANES, TM).astype(BF16)
    vtb_ref[0, :, 0, :HALF, :] = tt[d + hd:2 * d].reshape(B_HEADS, HALF, TM).astype(BF16)
    vta_ref[0, :, 0, LANES:, :] = jnp.ones((A_HEADS, ONES_ROWS, TM), BF16)
    vtb_ref[0, :, 0, HALF:, :] = jnp.ones((B_HEADS, ONES_ROWS, TM), BF16)


def _proj_call(x, shift, scale, wrow, wt, bfb):
    b, s, d = x.shape
    nt = s // TM
    nrow = wrow.shape[1]
    return pl.pallas_call(
        _proj_kernel,
        grid=(b, nt),
        in_specs=[
            pl.BlockSpec((1, TM, d), lambda i, t: (i, t, 0)),
            pl.BlockSpec((1, 1, d), lambda i, t: (i, 0, 0)),
            pl.BlockSpec((1, 1, d), lambda i, t: (i, 0, 0)),
            pl.BlockSpec((d, nrow), lambda i, t: (0, 0)),
            pl.BlockSpec((wt.shape[0], d), lambda i, t: (0, 0)),
            pl.BlockSpec((B_HEADS, 1), lambda i, t: (0, 0)),
        ],
        out_specs=(
            pl.BlockSpec((1, TM, d), lambda i, t: (i, t, 0)),
            pl.BlockSpec((1, TM, d), lambda i, t: (i, t, 0)),
            pl.BlockSpec((1, d, TM), lambda i, t: (i, 0, t)),
            pl.BlockSpec((1, A_HEADS, 1, VA_ROWS, TM), lambda i, t: (i, 0, t, 0, 0)),
            pl.BlockSpec((1, B_HEADS, 1, VB_ROWS, TM), lambda i, t: (i, 0, t, 0, 0)),
            pl.BlockSpec((1, TM, LANES), lambda i, t: (i, t, 0)),
            pl.BlockSpec((1, B_HEADS, 1, 8, LANES), lambda i, t: (i, 0, t, 0, 0)),
            pl.BlockSpec((1, B_HEADS, 1, 8, LANES), lambda i, t: (i, 0, t, 0, 0)),
        ),
        out_shape=(
            jax.ShapeDtypeStruct((b, s, d), BF16),
            jax.ShapeDtypeStruct((b, s, d), BF16),
            jax.ShapeDtypeStruct((b, d, s), BF16),
            jax.ShapeDtypeStruct((b, A_HEADS, nt, VA_ROWS, TM), BF16),
            jax.ShapeDtypeStruct((b, B_HEADS, nt, VB_ROWS, TM), BF16),
            jax.ShapeDtypeStruct((b, s, LANES), BF16),
            jax.ShapeDtypeStruct((b, B_HEADS, nt, 8, LANES), F32),
            jax.ShapeDtypeStruct((b, B_HEADS, nt, 8, LANES), F32),
        ),
        scratch_shapes=[pltpu.VMEM((B_HEADS, LANES), F32)],
        compiler_params=pltpu.CompilerParams(
            dimension_semantics=("arbitrary", "arbitrary"), vmem_limit_bytes=VMEM_LIMIT),
        name="proj",
    )(x, shift, scale, wrow, wt, bfb)


def _attn_kernel(*refs, is_a):
    if is_a:
        (qt_ref, k_ref, vt_ref, g_ref, tab_ref, lam_ref, ng_ref,
         y_ref, qtb_ref, m_ref, acc_ref, s_ref, mt_ref) = refs
        e_ref = None
    else:
        (qt_ref, k_ref, e_ref, vt_ref, g_ref, tab_ref, kn_ref, nc_ref,
         y_ref, qtb_ref, m_ref, acc_ref, s_ref, mt_ref) = refs
    p = pl.program_id(1)
    qi = pl.program_id(2)

    qt = qt_ref[0].astype(F32)
    row = lax.broadcasted_iota(jnp.int32, (LANES, T), 0)
    q_lo = jnp.where(row < HALF, qt, 0.0)
    q_hi = jnp.where(row >= HALF, qt, 0.0)
    qtb_ref[:LANES, :] = jnp.concatenate([q_lo, q_hi], axis=1).astype(BF16)
    if not is_a:
        r2 = lax.broadcasted_iota(jnp.int32, (LANES, 2 * T), 0)
        c2 = lax.broadcasted_iota(jnp.int32, (LANES, 2 * T), 1)
        head = 2 * p + jnp.where(c2 >= T, 1, 0)
        sel = (r2 == head) | (r2 == head + B_HEADS) | (r2 == head + 2 * B_HEADS)
        qtb_ref[LANES:, :] = jnp.where(sel, 1.0, 0.0).astype(BF16)

    def scores(j):
        rows = pl.ds(pl.multiple_of(j * T, T), T)
        kt = k_ref[0, rows, :]
        if not is_a:
            kt = jnp.concatenate([kt, e_ref[0, rows, :]], axis=1)
        return jnp.dot(kt, qtb_ref[...], preferred_element_type=F32)

    def pv_dot(vts, pb):
        if is_a:
            return jnp.dot(vts[0], pb, preferred_element_type=F32)
        return jnp.concatenate(
            [jnp.dot(vts[0], pb[:, :T], preferred_element_type=F32),
             jnp.dot(vts[1], pb[:, T:], preferred_element_type=F32)], axis=1)

    def softmax_pv(vts, s, mt):
        m_old = m_ref[...]
        m_new = jnp.maximum(m_old, mt)
        alpha = jnp.exp2(m_old - m_new)
        pv = pv_dot(vts, jnp.exp2(s - m_new).astype(BF16))
        acc_ref[...] = alpha * acc_ref[...] + pv
        m_ref[...] = m_new

    def v_tile(j, valid=None):
        jc = jnp.maximum(j, 0)
        vts = [vt_ref[0, u, jc] for u in range(1 if is_a else 2)]
        if valid is not None:
            vts = [jnp.where(valid, vt, jnp.zeros_like(vt)) for vt in vts]
        return vts

    def scores_to(slot, j):
        s = scores(jnp.maximum(j, 0))
        s_ref[slot] = s
        mt_ref[slot] = jnp.max(s, axis=0, keepdims=True)

    nblk = T // BLK
    tab_sub = tab_ref[0, 0] if is_a else None
    tab_diag = tab_ref[0, 1]

    def diag_tile(s):
        pbs, mts = [], []
        for lb in range(2 * nblk):
            qb = lb % nblk
            lanes = slice(lb * BLK, (lb + 1) * BLK)
            parts = [s[:(qb - 1) * BLK, lanes]] if qb >= 2 else []
            if qb >= 1:
                below = s[(qb - 1) * BLK:qb * BLK, lanes]
                parts.append(below if tab_sub is None else below + tab_sub)
            parts.append(s[qb * BLK:(qb + 1) * BLK, lanes] + tab_diag)
            sb = parts[0] if len(parts) == 1 else jnp.concatenate(parts, axis=0)
            mt = jnp.max(sb, axis=0, keepdims=True)
            pb = jnp.exp2(sb - mt).astype(BF16)
            if qb < nblk - 1:
                pb = jnp.concatenate([pb, jnp.zeros((T - (qb + 1) * BLK, BLK), BF16)], axis=0)
            pbs.append(pb)
            mts.append(mt)
        acc_ref[...] = pv_dot(v_tile(qi), jnp.concatenate(pbs, axis=1))
        m_ref[...] = jnp.concatenate(mts, axis=1)

    s_diag = scores(qi)
    if is_a:
        s_near = scores(jnp.maximum(qi - 1, 0))
        last = s_near[T - BLK:]
        last = jnp.concatenate(
            [last[:, :BLK] + tab_sub, last[:, BLK:T], last[:, T:T + BLK] + tab_sub,
             last[:, T + BLK:]], axis=1)
        s_near = jnp.concatenate([s_near[:T - BLK], last], axis=0)
        diag_tile(s_diag)
        top = qi - 2
        scores_to(0, top)
        softmax_pv(v_tile(qi - 1, qi >= 1), s_near, jnp.max(s_near, axis=0, keepdims=True))
        n_far = jnp.maximum(qi - 1, 0)
    else:
        top = qi - 1
        scores_to(0, top)
        diag_tile(s_diag)
        need = None
        for half, qh in enumerate((q_lo, q_hi)):
            qn = jnp.sqrt(jnp.max(jnp.sum(qh * qh, axis=0, keepdims=True), axis=1, keepdims=True))
            m_min = jnp.min(m_ref[:, half * T:(half + 1) * T], axis=1, keepdims=True)
            kn = jnp.max(kn_ref[0, half], axis=0)
            bound = (SKIP_NORM_MARGIN * qn) * kn + SKIP_ABS_MARGIN
            nd = (nc_ref[0, half] + bound) > (m_min - SKIP_THRESH)
            need = nd if need is None else (need | nd)
        tiles = need.shape[0]
        jidx = lax.broadcasted_iota(jnp.int32, need.shape, 0)
        cand = jnp.where(need & (jidx < qi), jidx, qi)
        n_far = qi - jnp.min(cand.reshape(tiles * 8, LANES))

    def far_steps(t0, count):
        for u in range(count):
            t = t0 + u
            scores_to((u + 1) % 2, top - t - 1)
            valid = None if u % 2 == 0 else t < n_far
            softmax_pv(v_tile(top - t, valid), s_ref[u % 2], mt_ref[u % 2])

    done = 0
    left = n_far + (n_far & 1)
    for unroll in FAR_UNROLLS:
        trips = left // unroll

        def body(i, carry, unroll=unroll, done=done):
            far_steps(done + unroll * i, unroll)
            return carry

        lax.fori_loop(0, trips, body, 0)
        done = done + unroll * trips
        left = left - unroll * trips

    vd = acc_ref.shape[0] - ONES_ROWS
    o = acc_ref[:vd] * (1.0 / acc_ref[vd:vd + 1])
    if is_a:
        lam = lam_ref[0:1, 0:1]
        o2 = o[:, :T] - lam * o[:, T:]
        ms = jnp.mean(o2 * o2, axis=0, keepdims=True)
        o2 = o2 * lax.rsqrt(ms + LN_EPS) * ng_ref[...] * (1.0 - LAM_INIT)
    else:
        o2 = jnp.concatenate([o[:, :T], o[:, T:]], axis=0)
    y = o2.T * g_ref[0].astype(F32)
    y_ref[0] = y.astype(BF16)


def _attn_call(qt, k, e, vt5, g, tab, lam, ng, kn, nc, *, is_a):
    b, d, s = qt.shape
    nq = s // T
    off = 0 if is_a else PAIRS
    kd = LANES if is_a else 2 * LANES
    v_rows = vt5.shape[3]
    in_specs = [
        pl.BlockSpec((1, LANES, T), lambda i, p, q: (i, p + off, q)),
        pl.BlockSpec((1, s, LANES), lambda i, p, q: (i, 0, p + off)),
    ]
    args = [qt, k]
    if not is_a:
        in_specs.append(pl.BlockSpec((1, s, LANES), lambda i, p, q: (i, 0, 0)))
        args.append(e)
    in_specs += [
        pl.BlockSpec((1, 1 if is_a else 2, s // TM, v_rows, TM), lambda i, p, q: (i, p, 0, 0, 0)),
        pl.BlockSpec((1, T, LANES), lambda i, p, q: (i, q, p + off)),
    ]
    args += [vt5, g]
    if is_a:
        in_specs.append(pl.BlockSpec((1, 2, BLK, BLK), lambda i, p, q: (p, 0, 0, 0)))
        in_specs.append(pl.BlockSpec((8, LANES), lambda i, p, q: (0, 0)))
        in_specs.append(pl.BlockSpec((LANES, 1), lambda i, p, q: (0, 0)))
        args += [tab, lam, ng]
    else:
        in_specs.append(pl.BlockSpec((1, 2, BLK, BLK), lambda i, p, q: (A_HEADS, 0, 0, 0)))
        stat_spec = pl.BlockSpec((1, 2, s // TM, 8, LANES), lambda i, p, q: (i, p, 0, 0, 0))
        in_specs += [stat_spec, stat_spec]
        args += [tab, kn, nc]
    return pl.pallas_call(
        functools.partial(_attn_kernel, is_a=is_a),
        grid=(b, PAIRS, nq),
        in_specs=in_specs,
        out_specs=pl.BlockSpec((1, T, LANES), lambda i, p, q: (i, q, p)),
        out_shape=jax.ShapeDtypeStruct((b, s, PAIRS * LANES), BF16),
        scratch_shapes=[
            pltpu.VMEM((kd, 2 * T), BF16),
            pltpu.VMEM((1, 2 * T), F32),
            pltpu.VMEM((v_rows, 2 * T), F32),
            pltpu.VMEM((2, T, 2 * T), F32),
            pltpu.VMEM((2, 1, 2 * T), F32),
        ],
        compiler_params=pltpu.CompilerParams(
            dimension_semantics=("arbitrary", "arbitrary", "arbitrary"),
            vmem_limit_bytes=VMEM_LIMIT),
        name="attn_a" if is_a else "attn_b",
    )(*args)


def _out_kernel(ya_ref, yb_ref, w1_ref, w2_ref, x_ref, gate_ref, lng_ref, lnb_ref, o_ref):
    y = (jnp.dot(ya_ref[0], w1_ref[...], preferred_element_type=F32)
         + jnp.dot(yb_ref[0], w2_ref[...], preferred_element_type=F32))
    z = DEEPNORM_ALPHA * x_ref[0] + gate_ref[0] * y
    mu = jnp.mean(z, axis=-1, keepdims=True)
    zc = z - mu
    var = jnp.mean(zc * zc, axis=-1, keepdims=True)
    o_ref[0] = zc * lax.rsqrt(var + LN_EPS) * lng_ref[...] + lnb_ref[...]


def _out_call(ya, yb, w1, w2, x, gate, lng, lnb):
    b, s, d = x.shape
    half = ya.shape[2]
    return pl.pallas_call(
        _out_kernel,
        grid=(b, s // TO),
        in_specs=[
            pl.BlockSpec((1, TO, half), lambda i, t: (i, t, 0)),
            pl.BlockSpec((1, TO, half), lambda i, t: (i, t, 0)),
            pl.BlockSpec((half, d), lambda i, t: (0, 0)),
            pl.BlockSpec((half, d), lambda i, t: (0, 0)),
            pl.BlockSpec((1, TO, d), lambda i, t: (i, t, 0)),
            pl.BlockSpec((1, 1, d), lambda i, t: (i, 0, 0)),
            pl.BlockSpec((1, d), lambda i, t: (0, 0)),
            pl.BlockSpec((1, d), lambda i, t: (0, 0)),
        ],
        out_specs=pl.BlockSpec((1, TO, d), lambda i, t: (i, t, 0)),
        out_shape=jax.ShapeDtypeStruct((b, s, d), x.dtype),
        compiler_params=pltpu.CompilerParams(
            dimension_semantics=("arbitrary", "arbitrary"), vmem_limit_bytes=VMEM_LIMIT),
        name="out",
    )(ya, yb, w1, w2, x, gate, lng, lnb)


def kernel(x, c, w_in, w_out, rel_bias, lam_q1, lam_k1, lam_q2, lam_k2, diff_norm_g,
           b_forget, w_ada, b_ada, ln_g, ln_b):
    b, s, d = x.shape
    assert s % T == 0 and s % TO == 0 and T == TM and d == PAIRS * 2 * LANES
    layer = 0
    aw = PAIRS * LANES

    c8 = jnp.pad(c, ((0, 8 - b), (0, 0)))
    ada8, lam = _ada_call(c8, w_ada[layer], b_ada[layer][None], lam_q1[layer][None],
                          lam_k1[layer][None], lam_q2[layer][None], lam_k2[layer][None])
    ada = ada8[:b]
    shift = ada[:, None, :d]
    scale = ada[:, None, d:2 * d]
    gate = ada[:, None, 2 * d:]

    w = w_in[layer]
    grp = lambda i: w[:, i * aw:(i + 1) * aw]
    wbf = jnp.pad(w[:, 8 * aw:], ((0, 0), (0, B_HEADS)))
    wrow = jnp.concatenate([grp(1), grp(5), grp(3), grp(7)], axis=1).astype(BF16)
    wt = jnp.concatenate([grp(0), grp(4), grp(2), grp(6), wbf], axis=1).T.astype(BF16)
    bfb = b_forget[layer].reshape(B_HEADS, 1).astype(F32)

    k, g, qt, vta, vtb, e, kn, nc = _proj_call(x, shift, scale, wrow, wt, bfb)
    tab = _table_call(rel_bias)
    ng = diff_norm_g[layer].reshape(LANES, 1)
    ya = _attn_call(qt, k, None, vta, g, tab, lam, ng, None, None, is_a=True)
    yb = _attn_call(qt, k, e, vtb, g, tab, None, None, kn, nc, is_a=False)

    wo = w_out[layer].astype(BF16)
    return _out_call(ya, yb, wo[:aw], wo[aw:], x, gate, ln_g[layer][None], ln_b[layer][None])
```

```python
import functools
import math

import jax
import jax.numpy as jnp
from jax import lax
from jax.experimental import pallas as pl
from jax.experimental.pallas import tpu as pltpu

F32 = jnp.float32
BF16 = jnp.bfloat16

LANES = 128
HALF = 64
A_HEADS = 4
B_HEADS = 8
PAIRS = 4
CHUNK = 64
N_BUCKETS = 32
MAX_DISTANCE = 128
LN_EPS = 1e-5
NEG_INF = -1e30
LOG2E = 1.4426950408889634
DEPTH = 1
DEEPNORM_ALPHA = (2 * DEPTH) ** 0.25
LAM_INIT = 0.8 - 0.6 * math.exp(-0.3 * 0)

T = 512
BLK = 128
TM = 512
TO = 2048
SKIP_THRESH = 152.0
SKIP_NORM_MARGIN = 1.02
SKIP_ABS_MARGIN = 0.5
FAR_UNROLLS = (8, 4, 2)
ONES_ROWS = 16
VA_ROWS = LANES + ONES_ROWS
VB_ROWS = HALF + ONES_ROWS
VMEM_LIMIT = 60 * 1024 * 1024


def _ada_kernel(c_ref, w_ref, b_ref, q1_ref, k1_ref, q2_ref, k2_ref, ada_ref, lam_ref):
    c = c_ref[...]
    sc = c * jax.nn.sigmoid(c)
    ada_ref[...] = jnp.dot(sc, w_ref[...], precision=lax.Precision.HIGHEST,
                           preferred_element_type=F32) + b_ref[...]
    s1 = jnp.sum(q1_ref[...] * k1_ref[...], axis=-1, keepdims=True)
    s2 = jnp.sum(q2_ref[...] * k2_ref[...], axis=-1, keepdims=True)
    lam = jnp.exp(s1) - jnp.exp(s2) + LAM_INIT
    lam_ref[...] = jnp.broadcast_to(lam, lam_ref.shape)


def _ada_call(c8, w_ada, b_ada, q1, k1, q2, k2):
    d3 = w_ada.shape[1]
    return pl.pallas_call(
        _ada_kernel,
        out_shape=(jax.ShapeDtypeStruct((c8.shape[0], d3), F32),
                   jax.ShapeDtypeStruct((8, LANES), F32)),
        compiler_params=pltpu.CompilerParams(vmem_limit_bytes=VMEM_LIMIT),
        name="ada",
    )(c8, w_ada, b_ada, q1, k1, q2, k2)


def _log_bucket_starts():
    nb = N_BUCKETS // 2
    e = nb // 2
    w = nb - e
    starts, n = [], e
    for j in range(1, w):
        while n ** w * e ** j < e ** w * MAX_DISTANCE ** j:
            n += 1
        starts.append(n)
    return starts


def _t5_bucket(rel):
    nb = N_BUCKETS // 2
    ret = jnp.where(rel > 0, nb, 0)
    n = jnp.abs(rel)
    max_exact = nb // 2
    large = jnp.full_like(n, max_exact)
    for start in _log_bucket_starts():
        large = large + jnp.where(n >= start, 1, 0)
    return ret + jnp.where(n < max_exact, n, large)


def _table_kernel(rb_ref, tab_ref):
    i = pl.program_id(0)
    h = jnp.minimum(i, A_HEADS - 1)
    kl = lax.broadcasted_iota(jnp.int32, (BLK, BLK), 0)
    ql = lax.broadcasted_iota(jnp.int32, (BLK, BLK), 1)
    far = rb_ref[N_BUCKETS // 2 - 1, h]
    for d in range(2):
        bucket = _t5_bucket(kl - ql + (d - 1) * BLK)
        bias = jnp.zeros((BLK, BLK), F32)
        for bk in range(N_BUCKETS):
            bias = jnp.where(bucket == bk, rb_ref[bk, h], bias)
        tab_a = (bias - far) * LOG2E
        if d == 1:
            tab_a = jnp.where((kl // CHUNK) <= (ql // CHUNK), tab_a, NEG_INF)
            tab_b = jnp.where(kl <= ql, 0.0, NEG_INF)
        else:
            tab_b = jnp.zeros((BLK, BLK), F32)
        tab_ref[0, d] = jnp.where(i < A_HEADS, tab_a, tab_b)


def _table_call(rel_bias):
    return pl.pallas_call(
        _table_kernel,
        grid=(A_HEADS + 1,),
        in_specs=[pl.BlockSpec(memory_space=pltpu.SMEM)],
        out_specs=pl.BlockSpec((1, 2, BLK, BLK), lambda i: (i, 0, 0, 0)),
        out_shape=jax.ShapeDtypeStruct((A_HEADS + 1, 2, BLK, BLK), F32),
        compiler_params=pltpu.CompilerParams(
            dimension_semantics=("arbitrary",), vmem_limit_bytes=VMEM_LIMIT),
        name="tables",
    )(rel_bias)


def _split3(v):
    hi = v.astype(BF16)
    r1 = v - hi.astype(F32)
    mid = r1.astype(BF16)
    lo = (r1 - mid.astype(F32)).astype(BF16)
    return hi, mid, lo


def _proj_kernel(x_ref, sh_ref, sc_ref, wrow_ref, wt_ref, bfb_ref,
                 k_ref, g_ref, qt_ref, vta_ref, vtb_ref, e_ref, kn_ref, nc_ref, carry_ref):
    t = pl.program_id(1)
    d = x_ref.shape[2]
    x = x_ref[0]
    mu = jnp.mean(x, axis=-1, keepdims=True)
    xc = x - mu
    var = jnp.mean(xc * xc, axis=-1, keepdims=True)
    h = (xc * lax.rsqrt(var + LN_EPS)) * (1.0 + sc_ref[0]) + sh_ref[0]
    hb = h.astype(BF16)

    r = jnp.dot(hb, wrow_ref[...], preferred_element_type=F32)
    kb = r[:, :d].astype(BF16)
    k_ref[0] = kb
    g = r[:, d:]
    g_ref[0] = (g * jax.nn.sigmoid(g)).astype(BF16)

    tt = lax.dot_general(wt_ref[...], hb, (((1,), (1,)), ((), ())),
                         preferred_element_type=F32)
    qt_ref[0] = (tt[:d] * (LOG2E * HALF ** -0.5)).astype(BF16)

    z = tt[2 * d:2 * d + B_HEADS] + bfb_ref[...]
    logf = jnp.minimum(z, 0.0) - jnp.log1p(jnp.exp(-jnp.abs(z)))
    ri = lax.broadcasted_iota(jnp.int32, (TM, TM), 0)
    ci = lax.broadcasted_iota(jnp.int32, (TM, TM), 1)
    upper = jnp.where(ri <= ci, 1.0, 0.0).astype(BF16)
    hi, mid, lo = _split3(logf)
    pieces = jnp.concatenate([hi.astype(F32), mid.astype(F32), lo.astype(F32),
                              jnp.zeros((B_HEADS, TM), F32)], axis=0)
    cs = jnp.dot(pieces.astype(BF16), upper, preferred_element_type=F32)

    @pl.when(t == 0)
    def _():
        carry_ref[...] = jnp.zeros_like(carry_ref)

    carry = jnp.concatenate([carry_ref[...]] * (TM // LANES), axis=1)
    cf = (cs[:B_HEADS] + cs[B_HEADS:2 * B_HEADS]) + cs[2 * B_HEADS:3 * B_HEADS] + carry
    carry_ref[...] = jnp.broadcast_to(cf[:, TM - 1:TM], carry_ref.shape)
    ncf = -cf * LOG2E
    vh, vm, vl = _split3(ncf)
    pt = jnp.concatenate([vh.astype(F32), vm.astype(F32), vl.astype(F32),
                          jnp.zeros((LANES - 3 * B_HEADS, TM), F32)], axis=0)
    e_ref[0] = pt.T.astype(BF16)

    kf = kb[:, d // 2:].astype(F32)
    gc = lax.broadcasted_iota(jnp.int32, (d // 2, LANES), 0)
    gh = lax.broadcasted_iota(jnp.int32, (d // 2, LANES), 1)
    gsel = jnp.where(gc // HALF == gh, 1.0, 0.0).astype(BF16)
    ksq = jnp.dot((kf * kf).astype(BF16), gsel, preferred_element_type=F32)
    knt = jnp.sqrt(jnp.max(ksq, axis=0, keepdims=True))
    nct = jnp.max(ncf, axis=1, keepdims=True)
    for hh in range(B_HEADS):
        kn_ref[0, hh, 0] = jnp.broadcast_to(knt[:, hh:hh + 1], (8, LANES))
        nc_ref[0, hh, 0] = jnp.broadcast_to(nct[hh:hh + 1, :], (8, LANES))

    hd = d // 2
    vta_ref[0, :, 0, :LANES, :] = tt[d:d + hd].reshape(A_HEADS, LANES, TM).astype(BF16)
    vtb_ref[0, :, 0, :HALF, :] = tt[d + hd:2 * d].reshape(B_HEADS, HALF, TM).astype(BF16)
    vta_ref[0, :, 0, LANES:, :] = jnp.ones((A_HEADS, ONES_ROWS, TM), BF16)
    vtb_ref[0, :, 0, HALF:, :] = jnp.ones((B_HEADS, ONES_ROWS, TM), BF16)


def _proj_call(x, shift, scale, wrow, wt, bfb):
    b, s, d = x.shape
    nt = s // TM
    nrow = wrow.shape[1]
    return pl.pallas_call(
        _proj_kernel,
        grid=(b, nt),
        in_specs=[
            pl.BlockSpec((1, TM, d), lambda i, t: (i, t, 0)),
            pl.BlockSpec((1, 1, d), lambda i, t: (i, 0, 0)),
            pl.BlockSpec((1, 1, d), lambda i, t: (i, 0, 0)),
            pl.BlockSpec((d, nrow), lambda i, t: (0, 0)),
            pl.BlockSpec((wt.shape[0], d), lambda i, t: (0, 0)),
            pl.BlockSpec((B_HEADS, 1), lambda i, t: (0, 0)),
        ],
        out_specs=(
            pl.BlockSpec((1, TM, d), lambda i, t: (i, t, 0)),
            pl.BlockSpec((1, TM, d), lambda i, t: (i, t, 0)),
            pl.BlockSpec((1, d, TM), lambda i, t: (i, 0, t)),
            pl.BlockSpec((1, A_HEADS, 1, VA_ROWS, TM), lambda i, t: (i, 0, t, 0, 0)),
            pl.BlockSpec((1, B_HEADS, 1, VB_ROWS, TM), lambda i, t: (i, 0, t, 0, 0)),
            pl.BlockSpec((1, TM, LANES), lambda i, t: (i, t, 0)),
            pl.BlockSpec((1, B_HEADS, 1, 8, LANES), lambda i, t: (i, 0, t, 0, 0)),
            pl.BlockSpec((1, B_HEADS, 1, 8, LANES), lambda i, t: (i, 0, t, 0, 0)),
        ),
        out_shape=(
            jax.ShapeDtypeStruct((b, s, d), BF16),
            jax.ShapeDtypeStruct((b, s, d), BF16),
            jax.ShapeDtypeStruct((b, d, s), BF16),
            jax.ShapeDtypeStruct((b, A_HEADS, nt, VA_ROWS, TM), BF16),
            jax.ShapeDtypeStruct((b, B_HEADS, nt, VB_ROWS, TM), BF16),
            jax.ShapeDtypeStruct((b, s, LANES), BF16),
            jax.ShapeDtypeStruct((b, B_HEADS, nt, 8, LANES), F32),
            jax.ShapeDtypeStruct((b, B_HEADS, nt, 8, LANES), F32),
        ),
        scratch_shapes=[pltpu.VMEM((B_HEADS, LANES), F32)],
        compiler_params=pltpu.CompilerParams(
            dimension_semantics=("arbitrary", "arbitrary"), vmem_limit_bytes=VMEM_LIMIT),
        name="proj",
    )(x, shift, scale, wrow, wt, bfb)


def _attn_kernel(*refs, is_a):
    if is_a:
        (qt_ref, k_ref, vt_ref, g_ref, tab_ref, lam_ref, ng_ref,
         y_ref, qtb_ref, m_ref, acc_ref, s_ref, mt_ref) = refs
        e_ref = None
    else:
        (qt_ref, k_ref, e_ref, vt_ref, g_ref, tab_ref, kn_ref, nc_ref,
         y_ref, qtb_ref, m_ref, acc_ref, s_ref, mt_ref) = refs
    p = pl.program_id(1)
    qi = pl.program_id(2)

    qt = qt_ref[0].astype(F32)
    row = lax.broadcasted_iota(jnp.int32, (LANES, T), 0)
    q_lo = jnp.where(row < HALF, qt, 0.0)
    q_hi = jnp.where(row >= HALF, qt, 0.0)
    qtb_ref[:LANES, :] = jnp.concatenate([q_lo, q_hi], axis=1).astype(BF16)
    if not is_a:
        r2 = lax.broadcasted_iota(jnp.int32, (LANES, 2 * T), 0)
        c2 = lax.broadcasted_iota(jnp.int32, (LANES, 2 * T), 1)
        head = 2 * p + jnp.where(c2 >= T, 1, 0)
        sel = (r2 == head) | (r2 == head + B_HEADS) | (r2 == head + 2 * B_HEADS)
        qtb_ref[LANES:, :] = jnp.where(sel, 1.0, 0.0).astype(BF16)

    def scores(j):
        rows = pl.ds(pl.multiple_of(j * T, T), T)
        kt = k_ref[0, rows, :]
        if not is_a:
            kt = jnp.concatenate([kt, e_ref[0, rows, :]], axis=1)
        return jnp.dot(kt, qtb_ref[...], preferred_element_type=F32)

    def pv_dot(vts, pb):
        if is_a:
            return jnp.dot(vts[0], pb, preferred_element_type=F32)
        return jnp.concatenate(
            [jnp.dot(vts[0], pb[:, :T], preferred_element_type=F32),
             jnp.dot(vts[1], pb[:, T:], preferred_element_type=F32)], axis=1)

    def softmax_pv(vts, s, mt):
        m_old = m_ref[...]
        m_new = jnp.maximum(m_old, mt)
        alpha = jnp.exp2(m_old - m_new)
        pv = pv_dot(vts, jnp.exp2(s - m_new).astype(BF16))
        acc_ref[...] = alpha * acc_ref[...] + pv
        m_ref[...] = m_new

    def v_tile(j, valid=None):
        jc = jnp.maximum(j, 0)
        vts = [vt_ref[0, u, jc] for u in range(1 if is_a else 2)]
        if valid is not None:
            vts = [jnp.where(valid, vt, jnp.zeros_like(vt)) for vt in vts]
        return vts

    def scores_to(slot, j):
        s = scores(jnp.maximum(j, 0))
        s_ref[slot] = s
        mt_ref[slot] = jnp.max(s, axis=0, keepdims=True)

    nblk = T // BLK
    tab_sub = tab_ref[0, 0] if is_a else None
    tab_diag = tab_ref[0, 1]

    def diag_tile(s):
        pbs, mts = [], []
        for lb in range(2 * nblk):
            qb = lb % nblk
            lanes = slice(lb * BLK, (lb + 1) * BLK)
            parts = [s[:(qb - 1) * BLK, lanes]] if qb >= 2 else []
            if qb >= 1:
                below = s[(qb - 1) * BLK:qb * BLK, lanes]
                parts.append(below if tab_sub is None else below + tab_sub)
            parts.append(s[qb * BLK:(qb + 1) * BLK, lanes] + tab_diag)
            sb = parts[0] if len(parts) == 1 else jnp.concatenate(parts, axis=0)
            mt = jnp.max(sb, axis=0, keepdims=True)
            pb = jnp.exp2(sb - mt).astype(BF16)
            if qb < nblk - 1:
                pb = jnp.concatenate([pb, jnp.zeros((T - (qb + 1) * BLK, BLK), BF16)], axis=0)
            pbs.append(pb)
            mts.append(mt)
        acc_ref[...] = pv_dot(v_tile(qi), jnp.concatenate(pbs, axis=1))
        m_ref[...] = jnp.concatenate(mts, axis=1)

    s_diag = scores(qi)
    if is_a:
        s_near = scores(jnp.maximum(qi - 1, 0)) + jnp.where(qi >= 1, 0.0, NEG_INF)
        last = s_near[T - BLK:]
        last = jnp.concatenate(
            [last[:, :BLK] + tab_sub, last[:, BLK:T], last[:, T:T + BLK] + tab_sub,
             last[:, T + BLK:]], axis=1)
        s_near = jnp.concatenate([s_near[:T - BLK], last], axis=0)
        diag_tile(s_diag)
        top = qi - 2
        scores_to(0, top)
        softmax_pv(v_tile(qi - 1), s_near, jnp.max(s_near, axis=0, keepdims=True))
        n_far = jnp.maximum(qi - 1, 0)
    else:
        top = qi - 1
        scores_to(0, top)
        diag_tile(s_diag)
        need = None
        for half, qh in enumerate((q_lo, q_hi)):
            qn = jnp.sqrt(jnp.max(jnp.sum(qh * qh, axis=0, keepdims=True), axis=1, keepdims=True))
            m_min = jnp.min(m_ref[:, half * T:(half + 1) * T], axis=1, keepdims=True)
            bound = (SKIP_NORM_MARGIN * qn) * kn_ref[0, half] + SKIP_ABS_MARGIN
            nd = (nc_ref[0, half] + bound) > (m_min - SKIP_THRESH)
            need = nd if need is None else (need | nd)
        tiles = need.shape[0]
        jidx = lax.broadcasted_iota(jnp.int32, need.shape, 0)
        cand = jnp.where(need & (jidx < qi), jidx, qi)
        n_far = qi - jnp.min(cand.reshape(tiles * 8, LANES))

    def far_steps(t0, count):
        for u in range(count):
            t = t0 + u
            scores_to((u + 1) % 2, top - t - 1)
            valid = None if u % 2 == 0 else t < n_far
            softmax_pv(v_tile(top - t, valid), s_ref[u % 2], mt_ref[u % 2])

    done = 0
    left = n_far + (n_far & 1)
    for unroll in FAR_UNROLLS:
        trips = left // unroll

        def body(i, carry, unroll=unroll, done=done):
            far_steps(done + unroll * i, unroll)
            return carry

        lax.fori_loop(0, trips, body, 0)
        done = done + unroll * trips
        left = left - unroll * trips

    vd = acc_ref.shape[0] - ONES_ROWS
    o = acc_ref[:vd] * (1.0 / acc_ref[vd:vd + 1])
    if is_a:
        lam = lam_ref[0:1, 0:1]
        o2 = o[:, :T] - lam * o[:, T:]
        ms = jnp.mean(o2 * o2, axis=0, keepdims=True)
        o2 = o2 * lax.rsqrt(ms + LN_EPS) * ng_ref[...] * (1.0 - LAM_INIT)
    else:
        o2 = jnp.concatenate([o[:, :T], o[:, T:]], axis=0)
    y = o2.T * g_ref[0].astype(F32)
    y_ref[0] = y.astype(BF16)


def _attn_call(qt, k, e, vt5, g, tab, lam, ng, kn, nc, *, is_a):
    b, d, s = qt.shape
    nq = s // T
    off = 0 if is_a else PAIRS
    kd = LANES if is_a else 2 * LANES
    v_rows = vt5.shape[3]
    in_specs = [
        pl.BlockSpec((1, LANES, T), lambda i, p, q: (i, p + off, q)),
        pl.BlockSpec((1, s, LANES), lambda i, p, q: (i, 0, p + off)),
    ]
    args = [qt, k]
    if not is_a:
        in_specs.append(pl.BlockSpec((1, s, LANES), lambda i, p, q: (i, 0, 0)))
        args.append(e)
    in_specs += [
        pl.BlockSpec((1, 1 if is_a else 2, s // TM, v_rows, TM), lambda i, p, q: (i, p, 0, 0, 0)),
        pl.BlockSpec((1, T, LANES), lambda i, p, q: (i, q, p + off)),
    ]
    args += [vt5, g]
    if is_a:
        in_specs.append(pl.BlockSpec((1, 2, BLK, BLK), lambda i, p, q: (p, 0, 0, 0)))
        in_specs.append(pl.BlockSpec((8, LANES), lambda i, p, q: (0, 0)))
        in_specs.append(pl.BlockSpec((LANES, 1), lambda i, p, q: (0, 0)))
        args += [tab, lam, ng]
    else:
        in_specs.append(pl.BlockSpec((1, 2, BLK, BLK), lambda i, p, q: (A_HEADS, 0, 0, 0)))
        stat_spec = pl.BlockSpec((1, 2, s // TM, 8, LANES), lambda i, p, q: (i, p, 0, 0, 0))
        in_specs += [stat_spec, stat_spec]
        args += [tab, kn, nc]
    return pl.pallas_call(
        functools.partial(_attn_kernel, is_a=is_a),
        grid=(b, PAIRS, nq),
        in_specs=in_specs,
        out_specs=pl.BlockSpec((1, T, LANES), lambda i, p, q: (i, q, p)),
        out_shape=jax.ShapeDtypeStruct((b, s, PAIRS * LANES), BF16),
        scratch_shapes=[
            pltpu.VMEM((kd, 2 * T), BF16),
            pltpu.VMEM((1, 2 * T), F32),
            pltpu.VMEM((v_rows, 2 * T), F32),
            pltpu.VMEM((2, T, 2 * T), F32),
            pltpu.VMEM((2, 1, 2 * T), F32),
        ],
        compiler_params=pltpu.CompilerParams(
            dimension_semantics=("arbitrary", "arbitrary", "arbitrary"),
            vmem_limit_bytes=VMEM_LIMIT),
        name="attn_a" if is_a else "attn_b",
    )(*args)


def _out_kernel(ya_ref, yb_ref, w1_ref, w2_ref, x_ref, gate_ref, lng_ref, lnb_ref, o_ref):
    y = (jnp.dot(ya_ref[0], w1_ref[...], preferred_element_type=F32)
         + jnp.dot(yb_ref[0], w2_ref[...], preferred_element_type=F32))
    z = DEEPNORM_ALPHA * x_ref[0] + gate_ref[0] * y
    mu = jnp.mean(z, axis=-1, keepdims=True)
    zc = z - mu
    var = jnp.mean(zc * zc, axis=-1, keepdims=True)
    o_ref[0] = zc * lax.rsqrt(var + LN_EPS) * lng_ref[...] + lnb_ref[...]


def _out_call(ya, yb, w1, w2, x, gate, lng, lnb):
    b, s, d = x.shape
    half = ya.shape[2]
    return pl.pallas_call(
        _out_kernel,
        grid=(b, s // TO),
        in_specs=[
            pl.BlockSpec((1, TO, half), lambda i, t: (i, t, 0)),
            pl.BlockSpec((1, TO, half), lambda i, t: (i, t, 0)),
            pl.BlockSpec((half, d), lambda i, t: (0, 0)),
            pl.BlockSpec((half, d), lambda i, t: (0, 0)),
            pl.BlockSpec((1, TO, d), lambda i, t: (i, t, 0)),
            pl.BlockSpec((1, 1, d), lambda i, t: (i, 0, 0)),
            pl.BlockSpec((1, d), lambda i, t: (0, 0)),
            pl.BlockSpec((1, d), lambda i, t: (0, 0)),
        ],
        out_specs=pl.BlockSpec((1, TO, d), lambda i, t: (i, t, 0)),
        out_shape=jax.ShapeDtypeStruct((b, s, d), x.dtype),
        compiler_params=pltpu.CompilerParams(
            dimension_semantics=("arbitrary", "arbitrary"), vmem_limit_bytes=VMEM_LIMIT),
        name="out",
    )(ya, yb, w1, w2, x, gate, lng, lnb)


def kernel(x, c, w_in, w_out, rel_bias, lam_q1, lam_k1, lam_q2, lam_k2, diff_norm_g,
           b_forget, w_ada, b_ada, ln_g, ln_b):
    b, s, d = x.shape
    assert s % T == 0 and s % TO == 0 and T == TM and d == PAIRS * 2 * LANES
    layer = 0
    aw = PAIRS * LANES

    c8 = jnp.pad(c, ((0, 8 - b), (0, 0)))
    ada8, lam = _ada_call(c8, w_ada[layer], b_ada[layer][None], lam_q1[layer][None],
                          lam_k1[layer][None], lam_q2[layer][None], lam_k2[layer][None])
    ada = ada8[:b]
    shift = ada[:, None, :d]
    scale = ada[:, None, d:2 * d]
    gate = ada[:, None, 2 * d:]

    w = w_in[layer]
    grp = lambda i: w[:, i * aw:(i + 1) * aw]
    wbf = jnp.pad(w[:, 8 * aw:], ((0, 0), (0, B_HEADS)))
    wrow = jnp.concatenate([grp(1), grp(5), grp(3), grp(7)], axis=1).astype(BF16)
    wt = jnp.concatenate([grp(0), grp(4), grp(2), grp(6), wbf], axis=1).T.astype(BF16)
    bfb = b_forget[layer].reshape(B_HEADS, 1).astype(F32)

    k, g, qt, vta, vtb, e, kn, nc = _proj_call(x, shift, scale, wrow, wt, bfb)
    tab = _table_call(rel_bias)
    ng = diff_norm_g[layer].reshape(LANES, 1)
    ya = _attn_call(qt, k, None, vta, g, tab, lam, ng, None, None, is_a=True)
    yb = _attn_call(qt, k, e, vtb, g, tab, None, None, kn, nc, is_a=False)

    wo = w_out[layer].astype(BF16)
    return _out_call(ya, yb, wo[:aw], wo[aw:], x, gate, ln_g[layer][None], ln_b[layer][None])
```

```python
import functools
import math

import jax
import jax.numpy as jnp
from jax import lax
from jax.experimental import pallas as pl
from jax.experimental.pallas import tpu as pltpu

F32 = jnp.float32
BF16 = jnp.bfloat16

LANES = 128
HALF = 64
A_HEADS = 4
B_HEADS = 8
PAIRS = 4
CHUNK = 64
N_BUCKETS = 32
MAX_DISTANCE = 128
LN_EPS = 1e-5
NEG_INF = -1e30
LOG2E = 1.4426950408889634
DEPTH = 1
DEEPNORM_ALPHA = (2 * DEPTH) ** 0.25
LAM_INIT = 0.8 - 0.6 * math.exp(-0.3 * 0)

T = 512
BLK = 128
TM = 512
TO = 2048
SKIP_THRESH = 152.0
SKIP_NORM_MARGIN = 1.02
SKIP_ABS_MARGIN = 0.5
FAR_UNROLLS = (8, 4, 2)
ONES_ROWS = 16
VA_ROWS = LANES + ONES_ROWS
VB_ROWS = HALF + ONES_ROWS
VMEM_LIMIT = 60 * 1024 * 1024


def _ada_kernel(c_ref, w_ref, b_ref, q1_ref, k1_ref, q2_ref, k2_ref, ada_ref, lam_ref):
    c = c_ref[...]
    sc = c * jax.nn.sigmoid(c)
    ada_ref[...] = jnp.dot(sc, w_ref[...], precision=lax.Precision.HIGHEST,
                           preferred_element_type=F32) + b_ref[...]
    s1 = jnp.sum(q1_ref[...] * k1_ref[...], axis=-1, keepdims=True)
    s2 = jnp.sum(q2_ref[...] * k2_ref[...], axis=-1, keepdims=True)
    lam = jnp.exp(s1) - jnp.exp(s2) + LAM_INIT
    lam_ref[...] = jnp.broadcast_to(lam, lam_ref.shape)


def _ada_call(c8, w_ada, b_ada, q1, k1, q2, k2):
    d3 = w_ada.shape[1]
    return pl.pallas_call(
        _ada_kernel,
        out_shape=(jax.ShapeDtypeStruct((c8.shape[0], d3), F32),
                   jax.ShapeDtypeStruct((8, LANES), F32)),
        compiler_params=pltpu.CompilerParams(vmem_limit_bytes=VMEM_LIMIT),
        name="ada",
    )(c8, w_ada, b_ada, q1, k1, q2, k2)


def _log_bucket_starts():
    nb = N_BUCKETS // 2
    e = nb // 2
    w = nb - e
    starts, n = [], e
    for j in range(1, w):
        while n ** w * e ** j < e ** w * MAX_DISTANCE ** j:
            n += 1
        starts.append(n)
    return starts


def _t5_bucket(rel):
    nb = N_BUCKETS // 2
    ret = jnp.where(rel > 0, nb, 0)
    n = jnp.abs(rel)
    max_exact = nb // 2
    large = jnp.full_like(n, max_exact)
    for start in _log_bucket_starts():
        large = large + jnp.where(n >= start, 1, 0)
    return ret + jnp.where(n < max_exact, n, large)


def _table_kernel(rb_ref, tab_ref):
    i = pl.program_id(0)
    h = jnp.minimum(i, A_HEADS - 1)
    kl = lax.broadcasted_iota(jnp.int32, (BLK, BLK), 0)
    ql = lax.broadcasted_iota(jnp.int32, (BLK, BLK), 1)
    far = rb_ref[N_BUCKETS // 2 - 1, h]
    for d in range(2):
        bucket = _t5_bucket(kl - ql + (d - 1) * BLK)
        bias = jnp.zeros((BLK, BLK), F32)
        for bk in range(N_BUCKETS):
            bias = jnp.where(bucket == bk, rb_ref[bk, h], bias)
        tab_a = (bias - far) * LOG2E
        if d == 1:
            tab_a = jnp.where((kl // CHUNK) <= (ql // CHUNK), tab_a, NEG_INF)
            tab_b = jnp.where(kl <= ql, 0.0, NEG_INF)
        else:
            tab_b = jnp.zeros((BLK, BLK), F32)
        tab_ref[0, d] = jnp.where(i < A_HEADS, tab_a, tab_b)


def _table_call(rel_bias):
    return pl.pallas_call(
        _table_kernel,
        grid=(A_HEADS + 1,),
        in_specs=[pl.BlockSpec(memory_space=pltpu.SMEM)],
        out_specs=pl.BlockSpec((1, 2, BLK, BLK), lambda i: (i, 0, 0, 0)),
        out_shape=jax.ShapeDtypeStruct((A_HEADS + 1, 2, BLK, BLK), F32),
        compiler_params=pltpu.CompilerParams(
            dimension_semantics=("arbitrary",), vmem_limit_bytes=VMEM_LIMIT),
        name="tables",
    )(rel_bias)


def _split3(v):
    hi = v.astype(BF16)
    r1 = v - hi.astype(F32)
    mid = r1.astype(BF16)
    lo = (r1 - mid.astype(F32)).astype(BF16)
    return hi, mid, lo


def _proj_kernel(x_ref, sh_ref, sc_ref, wrow_ref, wt_ref, bfb_ref,
                 k_ref, g_ref, qt_ref, vta_ref, vtb_ref, e_ref, kn_ref, nc_ref, carry_ref):
    t = pl.program_id(1)
    d = x_ref.shape[2]
    x = x_ref[0]
    mu = jnp.mean(x, axis=-1, keepdims=True)
    xc = x - mu
    var = jnp.mean(xc * xc, axis=-1, keepdims=True)
    h = (xc * lax.rsqrt(var + LN_EPS)) * (1.0 + sc_ref[0]) + sh_ref[0]
    hb = h.astype(BF16)

    kb = jnp.dot(hb, wrow_ref[...], preferred_element_type=F32).astype(BF16)
    k_ref[0] = kb

    tt = lax.dot_general(wt_ref[...], hb, (((1,), (1,)), ((), ())),
                         preferred_element_type=F32)
    qt_ref[0] = (tt[:d] * (LOG2E * HALF ** -0.5)).astype(BF16)
    g = tt[2 * d:3 * d]
    g_ref[0] = (g * jax.nn.sigmoid(g)).astype(BF16)

    z = tt[3 * d:3 * d + B_HEADS] + bfb_ref[...]
    logf = jnp.minimum(z, 0.0) - jnp.log1p(jnp.exp(-jnp.abs(z)))
    ri = lax.broadcasted_iota(jnp.int32, (TM, TM), 0)
    ci = lax.broadcasted_iota(jnp.int32, (TM, TM), 1)
    upper = jnp.where(ri <= ci, 1.0, 0.0).astype(BF16)
    hi, mid, lo = _split3(logf)
    pieces = jnp.concatenate([hi.astype(F32), mid.astype(F32), lo.astype(F32),
                              jnp.zeros((B_HEADS, TM), F32)], axis=0)
    cs = jnp.dot(pieces.astype(BF16), upper, preferred_element_type=F32)

    @pl.when(t == 0)
    def _():
        carry_ref[...] = jnp.zeros_like(carry_ref)

    carry = jnp.concatenate([carry_ref[...]] * (TM // LANES), axis=1)
    cf = (cs[:B_HEADS] + cs[B_HEADS:2 * B_HEADS]) + cs[2 * B_HEADS:3 * B_HEADS] + carry
    carry_ref[...] = jnp.broadcast_to(cf[:, TM - 1:TM], carry_ref.shape)
    ncf = -cf * LOG2E
    vh, vm, vl = _split3(ncf)
    pt = jnp.concatenate([vh.astype(F32), vm.astype(F32), vl.astype(F32),
                          jnp.zeros((LANES - 3 * B_HEADS, TM), F32)], axis=0)
    e_ref[0] = pt.T.astype(BF16)

    kf = kb[:, d // 2:].astype(F32)
    gc = lax.broadcasted_iota(jnp.int32, (d // 2, LANES), 0)
    gh = lax.broadcasted_iota(jnp.int32, (d // 2, LANES), 1)
    gsel = jnp.where(gc // HALF == gh, 1.0, 0.0).astype(BF16)
    ksq = jnp.dot((kf * kf).astype(BF16), gsel, preferred_element_type=F32)
    knt = jnp.sqrt(jnp.max(ksq, axis=0, keepdims=True))
    nct = jnp.max(ncf, axis=1, keepdims=True)
    for hh in range(B_HEADS):
        kn_ref[0, hh, 0] = jnp.broadcast_to(knt[:, hh:hh + 1], (8, LANES))
        nc_ref[0, hh, 0] = jnp.broadcast_to(nct[hh:hh + 1, :], (8, LANES))

    hd = d // 2
    vta_ref[0, :, 0, :LANES, :] = tt[d:d + hd].reshape(A_HEADS, LANES, TM).astype(BF16)
    vtb_ref[0, :, 0, :HALF, :] = tt[d + hd:2 * d].reshape(B_HEADS, HALF, TM).astype(BF16)
    vta_ref[0, :, 0, LANES:, :] = jnp.ones((A_HEADS, ONES_ROWS, TM), BF16)
    vtb_ref[0, :, 0, HALF:, :] = jnp.ones((B_HEADS, ONES_ROWS, TM), BF16)


def _proj_call(x, shift, scale, wrow, wt, bfb):
    b, s, d = x.shape
    nt = s // TM
    nrow = wrow.shape[1]
    return pl.pallas_call(
        _proj_kernel,
        grid=(b, nt),
        in_specs=[
            pl.BlockSpec((1, TM, d), lambda i, t: (i, t, 0)),
            pl.BlockSpec((1, 1, d), lambda i, t: (i, 0, 0)),
            pl.BlockSpec((1, 1, d), lambda i, t: (i, 0, 0)),
            pl.BlockSpec((d, nrow), lambda i, t: (0, 0)),
            pl.BlockSpec((wt.shape[0], d), lambda i, t: (0, 0)),
            pl.BlockSpec((B_HEADS, 1), lambda i, t: (0, 0)),
        ],
        out_specs=(
            pl.BlockSpec((1, TM, d), lambda i, t: (i, t, 0)),
            pl.BlockSpec((1, d, TM), lambda i, t: (i, 0, t)),
            pl.BlockSpec((1, d, TM), lambda i, t: (i, 0, t)),
            pl.BlockSpec((1, A_HEADS, 1, VA_ROWS, TM), lambda i, t: (i, 0, t, 0, 0)),
            pl.BlockSpec((1, B_HEADS, 1, VB_ROWS, TM), lambda i, t: (i, 0, t, 0, 0)),
            pl.BlockSpec((1, TM, LANES), lambda i, t: (i, t, 0)),
            pl.BlockSpec((1, B_HEADS, 1, 8, LANES), lambda i, t: (i, 0, t, 0, 0)),
            pl.BlockSpec((1, B_HEADS, 1, 8, LANES), lambda i, t: (i, 0, t, 0, 0)),
        ),
        out_shape=(
            jax.ShapeDtypeStruct((b, s, d), BF16),
            jax.ShapeDtypeStruct((b, d, s), BF16),
            jax.ShapeDtypeStruct((b, d, s), BF16),
            jax.ShapeDtypeStruct((b, A_HEADS, nt, VA_ROWS, TM), BF16),
            jax.ShapeDtypeStruct((b, B_HEADS, nt, VB_ROWS, TM), BF16),
            jax.ShapeDtypeStruct((b, s, LANES), BF16),
            jax.ShapeDtypeStruct((b, B_HEADS, nt, 8, LANES), F32),
            jax.ShapeDtypeStruct((b, B_HEADS, nt, 8, LANES), F32),
        ),
        scratch_shapes=[pltpu.VMEM((B_HEADS, LANES), F32)],
        compiler_params=pltpu.CompilerParams(
            dimension_semantics=("arbitrary", "arbitrary"), vmem_limit_bytes=VMEM_LIMIT),
        name="proj",
    )(x, shift, scale, wrow, wt, bfb)


def _attn_kernel(*refs, is_a):
    if is_a:
        (qt_ref, k_ref, vt_ref, g_ref, tab_ref, lam_ref, ng_ref,
         y_ref, qtb_ref, m_ref, acc_ref, s_ref, mt_ref) = refs
        e_ref = None
    else:
        (qt_ref, k_ref, e_ref, vt_ref, g_ref, tab_ref, kn_ref, nc_ref,
         y_ref, qtb_ref, m_ref, acc_ref, s_ref, mt_ref) = refs
    p = pl.program_id(1)
    qi = pl.program_id(2)

    qt = qt_ref[0].astype(F32)
    row = lax.broadcasted_iota(jnp.int32, (LANES, T), 0)
    q_lo = jnp.where(row < HALF, qt, 0.0)
    q_hi = jnp.where(row >= HALF, qt, 0.0)
    qtb_ref[:LANES, :] = jnp.concatenate([q_lo, q_hi], axis=1).astype(BF16)
    if not is_a:
        r2 = lax.broadcasted_iota(jnp.int32, (LANES, 2 * T), 0)
        c2 = lax.broadcasted_iota(jnp.int32, (LANES, 2 * T), 1)
        head = 2 * p + jnp.where(c2 >= T, 1, 0)
        sel = (r2 == head) | (r2 == head + B_HEADS) | (r2 == head + 2 * B_HEADS)
        qtb_ref[LANES:, :] = jnp.where(sel, 1.0, 0.0).astype(BF16)

    def scores(j):
        rows = pl.ds(pl.multiple_of(j * T, T), T)
        kt = k_ref[0, rows, :]
        if not is_a:
            kt = jnp.concatenate([kt, e_ref[0, rows, :]], axis=1)
        return jnp.dot(kt, qtb_ref[...], preferred_element_type=F32)

    def pv_dot(vts, pb):
        if is_a:
            return jnp.dot(vts[0], pb, preferred_element_type=F32)
        return jnp.concatenate(
            [jnp.dot(vts[0], pb[:, :T], preferred_element_type=F32),
             jnp.dot(vts[1], pb[:, T:], preferred_element_type=F32)], axis=1)

    def softmax_pv(vts, s, mt):
        m_old = m_ref[...]
        m_new = jnp.maximum(m_old, mt)
        alpha = jnp.exp2(m_old - m_new)
        pv = pv_dot(vts, jnp.exp2(s - m_new).astype(BF16))
        acc_ref[...] = alpha * acc_ref[...] + pv
        m_ref[...] = m_new

    def v_tile(j, valid=None):
        jc = jnp.maximum(j, 0)
        vts = [vt_ref[0, u, jc] for u in range(1 if is_a else 2)]
        if valid is not None:
            vts = [jnp.where(valid, vt, jnp.zeros_like(vt)) for vt in vts]
        return vts

    def scores_to(slot, j):
        s = scores(jnp.maximum(j, 0))
        s_ref[slot] = s
        mt_ref[slot] = jnp.max(s, axis=0, keepdims=True)

    nblk = T // BLK
    tab_sub = tab_ref[0, 0] if is_a else None
    tab_diag = tab_ref[0, 1]

    def diag_tile(s):
        pbs, mts = [], []
        for lb in range(2 * nblk):
            qb = lb % nblk
            lanes = slice(lb * BLK, (lb + 1) * BLK)
            parts = [s[:(qb - 1) * BLK, lanes]] if qb >= 2 else []
            if qb >= 1:
                below = s[(qb - 1) * BLK:qb * BLK, lanes]
                parts.append(below if tab_sub is None else below + tab_sub)
            parts.append(s[qb * BLK:(qb + 1) * BLK, lanes] + tab_diag)
            sb = parts[0] if len(parts) == 1 else jnp.concatenate(parts, axis=0)
            mt = jnp.max(sb, axis=0, keepdims=True)
            pb = jnp.exp2(sb - mt).astype(BF16)
            if qb < nblk - 1:
                pb = jnp.concatenate([pb, jnp.zeros((T - (qb + 1) * BLK, BLK), BF16)], axis=0)
            pbs.append(pb)
            mts.append(mt)
        acc_ref[...] = pv_dot(v_tile(qi), jnp.concatenate(pbs, axis=1))
        m_ref[...] = jnp.concatenate(mts, axis=1)

    s_diag = scores(qi)
    if is_a:
        s_near = scores(jnp.maximum(qi - 1, 0)) + jnp.where(qi >= 1, 0.0, NEG_INF)
        last = s_near[T - BLK:]
        last = jnp.concatenate(
            [last[:, :BLK] + tab_sub, last[:, BLK:T], last[:, T:T + BLK] + tab_sub,
             last[:, T + BLK:]], axis=1)
        s_near = jnp.concatenate([s_near[:T - BLK], last], axis=0)
        diag_tile(s_diag)
        top = qi - 2
        scores_to(0, top)
        softmax_pv(v_tile(qi - 1), s_near, jnp.max(s_near, axis=0, keepdims=True))
        n_far = jnp.maximum(qi - 1, 0)
    else:
        top = qi - 1
        scores_to(0, top)
        diag_tile(s_diag)
        need = None
        for half, qh in enumerate((q_lo, q_hi)):
            qn = jnp.sqrt(jnp.max(jnp.sum(qh * qh, axis=0, keepdims=True), axis=1, keepdims=True))
            m_min = jnp.min(m_ref[:, half * T:(half + 1) * T], axis=1, keepdims=True)
            bound = (SKIP_NORM_MARGIN * qn) * kn_ref[0, half] + SKIP_ABS_MARGIN
            nd = (nc_ref[0, half] + bound) > (m_min - SKIP_THRESH)
            need = nd if need is None else (need | nd)
        tiles = need.shape[0]
        jidx = lax.broadcasted_iota(jnp.int32, need.shape, 0)
        cand = jnp.where(need & (jidx < qi), jidx, qi)
        n_far = qi - jnp.min(cand.reshape(tiles * 8, LANES))

    def far_steps(t0, count):
        for u in range(count):
            t = t0 + u
            scores_to((u + 1) % 2, top - t - 1)
            valid = None if u % 2 == 0 else t < n_far
            softmax_pv(v_tile(top - t, valid), s_ref[u % 2], mt_ref[u % 2])

    done = 0
    left = n_far + (n_far & 1)
    for unroll in FAR_UNROLLS:
        trips = left // unroll

        def body(i, carry, unroll=unroll, done=done):
            far_steps(done + unroll * i, unroll)
            return carry

        lax.fori_loop(0, trips, body, 0)
        done = done + unroll * trips
        left = left - unroll * trips

    vd = acc_ref.shape[0] - ONES_ROWS
    o = acc_ref[:vd] * (1.0 / acc_ref[vd:vd + 1])
    if is_a:
        lam = lam_ref[0:1, 0:1]
        o2 = o[:, :T] - lam * o[:, T:]
        ms = jnp.mean(o2 * o2, axis=0, keepdims=True)
        o2 = o2 * lax.rsqrt(ms + LN_EPS) * ng_ref[...] * (1.0 - LAM_INIT)
    else:
        o2 = jnp.concatenate([o[:, :T], o[:, T:]], axis=0)
    y_ref[0] = (o2 * g_ref[0].astype(F32)).astype(BF16)


def _attn_call(qt, k, e, vt5, g, tab, lam, ng, kn, nc, *, is_a):
    b, d, s = qt.shape
    nq = s // T
    off = 0 if is_a else PAIRS
    kd = LANES if is_a else 2 * LANES
    v_rows = vt5.shape[3]
    in_specs = [
        pl.BlockSpec((1, LANES, T), lambda i, p, q: (i, p + off, q)),
        pl.BlockSpec((1, s, LANES), lambda i, p, q: (i, 0, p + off)),
    ]
    args = [qt, k]
    if not is_a:
        in_specs.append(pl.BlockSpec((1, s, LANES), lambda i, p, q: (i, 0, 0)))
        args.append(e)
    in_specs += [
        pl.BlockSpec((1, 1 if is_a else 2, s // TM, v_rows, TM), lambda i, p, q: (i, p, 0, 0, 0)),
        pl.BlockSpec((1, LANES, T), lambda i, p, q: (i, p + off, q)),
    ]
    args += [vt5, g]
    if is_a:
        in_specs.append(pl.BlockSpec((1, 2, BLK, BLK), lambda i, p, q: (p, 0, 0, 0)))
        in_specs.append(pl.BlockSpec((8, LANES), lambda i, p, q: (0, 0)))
        in_specs.append(pl.BlockSpec((LANES, 1), lambda i, p, q: (0, 0)))
        args += [tab, lam, ng]
    else:
        in_specs.append(pl.BlockSpec((1, 2, BLK, BLK), lambda i, p, q: (A_HEADS, 0, 0, 0)))
        stat_spec = pl.BlockSpec((1, 2, s // TM, 8, LANES), lambda i, p, q: (i, p, 0, 0, 0))
        in_specs += [stat_spec, stat_spec]
        args += [tab, kn, nc]
    return pl.pallas_call(
        functools.partial(_attn_kernel, is_a=is_a),
        grid=(b, PAIRS, nq),
        in_specs=in_specs,
        out_specs=pl.BlockSpec((1, LANES, T), lambda i, p, q: (i, p, q)),
        out_shape=jax.ShapeDtypeStruct((b, PAIRS * LANES, s), BF16),
        scratch_shapes=[
            pltpu.VMEM((kd, 2 * T), BF16),
            pltpu.VMEM((1, 2 * T), F32),
            pltpu.VMEM((v_rows, 2 * T), F32),
            pltpu.VMEM((2, T, 2 * T), F32),
            pltpu.VMEM((2, 1, 2 * T), F32),
        ],
        compiler_params=pltpu.CompilerParams(
            dimension_semantics=("arbitrary", "arbitrary", "arbitrary"),
            vmem_limit_bytes=VMEM_LIMIT),
        name="attn_a" if is_a else "attn_b",
    )(*args)


def _out_kernel(ya_ref, yb_ref, w1_ref, w2_ref, x_ref, gate_ref, lng_ref, lnb_ref, o_ref):
    tn = (((0,), (0,)), ((), ()))
    y = (lax.dot_general(ya_ref[0], w1_ref[...], tn, preferred_element_type=F32)
         + lax.dot_general(yb_ref[0], w2_ref[...], tn, preferred_element_type=F32))
    z = DEEPNORM_ALPHA * x_ref[0] + gate_ref[0] * y
    mu = jnp.mean(z, axis=-1, keepdims=True)
    zc = z - mu
    var = jnp.mean(zc * zc, axis=-1, keepdims=True)
    o_ref[0] = zc * lax.rsqrt(var + LN_EPS) * lng_ref[...] + lnb_ref[...]


def _out_call(ya, yb, w1, w2, x, gate, lng, lnb):
    b, s, d = x.shape
    half = ya.shape[1]
    return pl.pallas_call(
        _out_kernel,
        grid=(b, s // TO),
        in_specs=[
            pl.BlockSpec((1, half, TO), lambda i, t: (i, 0, t)),
            pl.BlockSpec((1, half, TO), lambda i, t: (i, 0, t)),
            pl.BlockSpec((half, d), lambda i, t: (0, 0)),
            pl.BlockSpec((half, d), lambda i, t: (0, 0)),
            pl.BlockSpec((1, TO, d), lambda i, t: (i, t, 0)),
            pl.BlockSpec((1, 1, d), lambda i, t: (i, 0, 0)),
            pl.BlockSpec((1, d), lambda i, t: (0, 0)),
            pl.BlockSpec((1, d), lambda i, t: (0, 0)),
        ],
        out_specs=pl.BlockSpec((1, TO, d), lambda i, t: (i, t, 0)),
        out_shape=jax.ShapeDtypeStruct((b, s, d), x.dtype),
        compiler_params=pltpu.CompilerParams(
            dimension_semantics=("arbitrary", "arbitrary"), vmem_limit_bytes=VMEM_LIMIT),
        name="out",
    )(ya, yb, w1, w2, x, gate, lng, lnb)


def kernel(x, c, w_in, w_out, rel_bias, lam_q1, lam_k1, lam_q2, lam_k2, diff_norm_g,
           b_forget, w_ada, b_ada, ln_g, ln_b):
    b, s, d = x.shape
    assert s % T == 0 and s % TO == 0 and T == TM and d == PAIRS * 2 * LANES
    layer = 0
    aw = PAIRS * LANES

    c8 = jnp.pad(c, ((0, 8 - b), (0, 0)))
    ada8, lam = _ada_call(c8, w_ada[layer], b_ada[layer][None], lam_q1[layer][None],
                          lam_k1[layer][None], lam_q2[layer][None], lam_k2[layer][None])
    ada = ada8[:b]
    shift = ada[:, None, :d]
    scale = ada[:, None, d:2 * d]
    gate = ada[:, None, 2 * d:]

    w = w_in[layer]
    grp = lambda i: w[:, i * aw:(i + 1) * aw]
    wbf = jnp.pad(w[:, 8 * aw:], ((0, 0), (0, B_HEADS)))
    wrow = jnp.concatenate([grp(1), grp(5)], axis=1).astype(BF16)
    wt = jnp.concatenate([grp(0), grp(4), grp(2), grp(6), grp(3), grp(7), wbf],
                         axis=1).T.astype(BF16)
    bfb = b_forget[layer].reshape(B_HEADS, 1).astype(F32)

    k, g, qt, vta, vtb, e, kn, nc = _proj_call(x, shift, scale, wrow, wt, bfb)
    tab = _table_call(rel_bias)
    ng = diff_norm_g[layer].reshape(LANES, 1)
    ya = _attn_call(qt, k, None, vta, g, tab, lam, ng, None, None, is_a=True)
    yb = _attn_call(qt, k, e, vtb, g, tab, None, None, kn, nc, is_a=False)

    wo = w_out[layer].astype(BF16)
    return _out_call(ya, yb, wo[:aw], wo[aw:], x, gate, ln_g[layer][None], ln_b[layer][None])
```

```python
import functools
import math

import jax
import jax.numpy as jnp
from jax import lax
from jax.experimental import pallas as pl
from jax.experimental.pallas import tpu as pltpu

F32 = jnp.float32
BF16 = jnp.bfloat16

LANES = 128
HALF = 64
A_HEADS = 4
B_HEADS = 8
PAIRS = 4
CHUNK = 64
N_BUCKETS = 32
MAX_DISTANCE = 128
LN_EPS = 1e-5
NEG_INF = -1e30
LOG2E = 1.4426950408889634
DEPTH = 1
DEEPNORM_ALPHA = (2 * DEPTH) ** 0.25
LAM_INIT = 0.8 - 0.6 * math.exp(-0.3 * 0)

T = 512
BLK = 128
TM = 512
TO = 2048
SKIP_THRESH = 152.0
SKIP_NORM_MARGIN = 1.02
SKIP_ABS_MARGIN = 0.5
FAR_UNROLLS = (8, 4, 2)
ONES_ROWS = 16
VA_ROWS = LANES + ONES_ROWS
VB_ROWS = HALF + ONES_ROWS
VMEM_LIMIT = 60 * 1024 * 1024


def _ada_kernel(c_ref, w_ref, b_ref, q1_ref, k1_ref, q2_ref, k2_ref, ada_ref, lam_ref):
    c = c_ref[...]
    sc = c * jax.nn.sigmoid(c)
    ada_ref[...] = jnp.dot(sc, w_ref[...], precision=lax.Precision.HIGHEST,
                           preferred_element_type=F32) + b_ref[...]
    s1 = jnp.sum(q1_ref[...] * k1_ref[...], axis=-1, keepdims=True)
    s2 = jnp.sum(q2_ref[...] * k2_ref[...], axis=-1, keepdims=True)
    lam = jnp.exp(s1) - jnp.exp(s2) + LAM_INIT
    lam_ref[...] = jnp.broadcast_to(lam, lam_ref.shape)


def _ada_call(c8, w_ada, b_ada, q1, k1, q2, k2):
    d3 = w_ada.shape[1]
    return pl.pallas_call(
        _ada_kernel,
        out_shape=(jax.ShapeDtypeStruct((c8.shape[0], d3), F32),
                   jax.ShapeDtypeStruct((8, LANES), F32)),
        compiler_params=pltpu.CompilerParams(vmem_limit_bytes=VMEM_LIMIT),
        name="ada",
    )(c8, w_ada, b_ada, q1, k1, q2, k2)


def _log_bucket_starts():
    nb = N_BUCKETS // 2
    e = nb // 2
    w = nb - e
    starts, n = [], e
    for j in range(1, w):
        while n ** w * e ** j < e ** w * MAX_DISTANCE ** j:
            n += 1
        starts.append(n)
    return starts


def _t5_bucket(rel):
    nb = N_BUCKETS // 2
    ret = jnp.where(rel > 0, nb, 0)
    n = jnp.abs(rel)
    max_exact = nb // 2
    large = jnp.full_like(n, max_exact)
    for start in _log_bucket_starts():
        large = large + jnp.where(n >= start, 1, 0)
    return ret + jnp.where(n < max_exact, n, large)


def _table_kernel(rb_ref, tab_ref):
    i = pl.program_id(0)
    h = jnp.minimum(i, A_HEADS - 1)
    kl = lax.broadcasted_iota(jnp.int32, (BLK, BLK), 0)
    ql = lax.broadcasted_iota(jnp.int32, (BLK, BLK), 1)
    far = rb_ref[N_BUCKETS // 2 - 1, h]
    for d in range(2):
        bucket = _t5_bucket(kl - ql + (d - 1) * BLK)
        bias = jnp.zeros((BLK, BLK), F32)
        for bk in range(N_BUCKETS):
            bias = jnp.where(bucket == bk, rb_ref[bk, h], bias)
        tab_a = (bias - far) * LOG2E
        if d == 1:
            tab_a = jnp.where((kl // CHUNK) <= (ql // CHUNK), tab_a, NEG_INF)
            tab_b = jnp.where(kl <= ql, 0.0, NEG_INF)
        else:
            tab_b = jnp.zeros((BLK, BLK), F32)
        tab_ref[0, d] = jnp.where(i < A_HEADS, tab_a, tab_b)


def _table_call(rel_bias):
    return pl.pallas_call(
        _table_kernel,
        grid=(A_HEADS + 1,),
        in_specs=[pl.BlockSpec(memory_space=pltpu.SMEM)],
        out_specs=pl.BlockSpec((1, 2, BLK, BLK), lambda i: (i, 0, 0, 0)),
        out_shape=jax.ShapeDtypeStruct((A_HEADS + 1, 2, BLK, BLK), F32),
        compiler_params=pltpu.CompilerParams(
            dimension_semantics=("arbitrary",), vmem_limit_bytes=VMEM_LIMIT),
        name="tables",
    )(rel_bias)


def _split3(v):
    hi = v.astype(BF16)
    r1 = v - hi.astype(F32)
    mid = r1.astype(BF16)
    lo = (r1 - mid.astype(F32)).astype(BF16)
    return hi, mid, lo


def _proj_kernel(x_ref, sh_ref, sc_ref, wrow_ref, wt_ref, bfb_ref,
                 k_ref, g_ref, qt_ref, vta_ref, vtb_ref, e_ref, kn_ref, nc_ref, carry_ref):
    t = pl.program_id(1)
    d = x_ref.shape[2]
    x = x_ref[0]
    mu = jnp.mean(x, axis=-1, keepdims=True)
    xc = x - mu
    var = jnp.mean(xc * xc, axis=-1, keepdims=True)
    h = (xc * lax.rsqrt(var + LN_EPS)) * (1.0 + sc_ref[0]) + sh_ref[0]
    hb = h.astype(BF16)

    kb = jnp.dot(hb, wrow_ref[...], preferred_element_type=F32).astype(BF16)
    k_ref[0] = kb

    tt = lax.dot_general(wt_ref[...], hb, (((1,), (1,)), ((), ())),
                         preferred_element_type=F32)
    qt_ref[0] = (tt[:d] * (LOG2E * HALF ** -0.5)).astype(BF16)
    g = tt[2 * d:3 * d]
    g_ref[0] = (g * jax.nn.sigmoid(g)).astype(BF16)

    z = tt[3 * d:3 * d + B_HEADS] + bfb_ref[...]
    logf = jnp.minimum(z, 0.0) - jnp.log1p(jnp.exp(-jnp.abs(z)))
    ri = lax.broadcasted_iota(jnp.int32, (TM, TM), 0)
    ci = lax.broadcasted_iota(jnp.int32, (TM, TM), 1)
    upper = jnp.where(ri <= ci, 1.0, 0.0).astype(BF16)
    hi, mid, lo = _split3(logf)
    pieces = jnp.concatenate([hi.astype(F32), mid.astype(F32), lo.astype(F32),
                              jnp.zeros((B_HEADS, TM), F32)], axis=0)
    cs = jnp.dot(pieces.astype(BF16), upper, preferred_element_type=F32)

    @pl.when(t == 0)
    def _():
        carry_ref[...] = jnp.zeros_like(carry_ref)

    carry = jnp.concatenate([carry_ref[...]] * (TM // LANES), axis=1)
    cf = (cs[:B_HEADS] + cs[B_HEADS:2 * B_HEADS]) + cs[2 * B_HEADS:3 * B_HEADS] + carry
    carry_ref[...] = jnp.broadcast_to(cf[:, TM - 1:TM], carry_ref.shape)
    ncf = -cf * LOG2E
    vh, vm, vl = _split3(ncf)
    pt = jnp.concatenate([vh.astype(F32), vm.astype(F32), vl.astype(F32),
                          jnp.zeros((LANES - 3 * B_HEADS, TM), F32)], axis=0)
    e_ref[0] = pt.T.astype(BF16)

    kf = kb[:, d // 2:].astype(F32)
    gc = lax.broadcasted_iota(jnp.int32, (d // 2, LANES), 0)
    gh = lax.broadcasted_iota(jnp.int32, (d // 2, LANES), 1)
    gsel = jnp.where(gc // HALF == gh, 1.0, 0.0).astype(BF16)
    ksq = jnp.dot((kf * kf).astype(BF16), gsel, preferred_element_type=F32)
    knt = jnp.sqrt(jnp.max(ksq, axis=0, keepdims=True))
    nct = jnp.max(ncf, axis=1, keepdims=True)
    for hh in range(B_HEADS):
        kn_ref[0, hh, 0] = jnp.broadcast_to(knt[:, hh:hh + 1], (8, LANES))
        nc_ref[0, hh, 0] = jnp.broadcast_to(nct[hh:hh + 1, :], (8, LANES))

    hd = d // 2
    vta_ref[0, :, 0, :LANES, :] = tt[d:d + hd].reshape(A_HEADS, LANES, TM).astype(BF16)
    vtb_ref[0, :, 0, :HALF, :] = tt[d + hd:2 * d].reshape(B_HEADS, HALF, TM).astype(BF16)
    vta_ref[0, :, 0, LANES:, :] = jnp.ones((A_HEADS, ONES_ROWS, TM), BF16)
    vtb_ref[0, :, 0, HALF:, :] = jnp.ones((B_HEADS, ONES_ROWS, TM), BF16)


def _proj_call(x, shift, scale, wrow, wt, bfb):
    b, s, d = x.shape
    nt = s // TM
    nrow = wrow.shape[1]
    return pl.pallas_call(
        _proj_kernel,
        grid=(b, nt),
        in_specs=[
            pl.BlockSpec((1, TM, d), lambda i, t: (i, t, 0)),
            pl.BlockSpec((1, 1, d), lambda i, t: (i, 0, 0)),
            pl.BlockSpec((1, 1, d), lambda i, t: (i, 0, 0)),
            pl.BlockSpec((d, nrow), lambda i, t: (0, 0)),
            pl.BlockSpec((wt.shape[0], d), lambda i, t: (0, 0)),
            pl.BlockSpec((B_HEADS, 1), lambda i, t: (0, 0)),
        ],
        out_specs=(
            pl.BlockSpec((1, TM, d), lambda i, t: (i, t, 0)),
            pl.BlockSpec((1, d, TM), lambda i, t: (i, 0, t)),
            pl.BlockSpec((1, d, TM), lambda i, t: (i, 0, t)),
            pl.BlockSpec((1, A_HEADS, 1, VA_ROWS, TM), lambda i, t: (i, 0, t, 0, 0)),
            pl.BlockSpec((1, B_HEADS, 1, VB_ROWS, TM), lambda i, t: (i, 0, t, 0, 0)),
            pl.BlockSpec((1, TM, LANES), lambda i, t: (i, t, 0)),
            pl.BlockSpec((1, B_HEADS, 1, 8, LANES), lambda i, t: (i, 0, t, 0, 0)),
            pl.BlockSpec((1, B_HEADS, 1, 8, LANES), lambda i, t: (i, 0, t, 0, 0)),
        ),
        out_shape=(
            jax.ShapeDtypeStruct((b, s, d), BF16),
            jax.ShapeDtypeStruct((b, d, s), BF16),
            jax.ShapeDtypeStruct((b, d, s), BF16),
            jax.ShapeDtypeStruct((b, A_HEADS, nt, VA_ROWS, TM), BF16),
            jax.ShapeDtypeStruct((b, B_HEADS, nt, VB_ROWS, TM), BF16),
            jax.ShapeDtypeStruct((b, s, LANES), BF16),
            jax.ShapeDtypeStruct((b, B_HEADS, nt, 8, LANES), F32),
            jax.ShapeDtypeStruct((b, B_HEADS, nt, 8, LANES), F32),
        ),
        scratch_shapes=[pltpu.VMEM((B_HEADS, LANES), F32)],
        compiler_params=pltpu.CompilerParams(
            dimension_semantics=("arbitrary", "arbitrary"), vmem_limit_bytes=VMEM_LIMIT),
        name="proj",
    )(x, shift, scale, wrow, wt, bfb)


def _attn_kernel(*refs, is_a):
    if is_a:
        (qt_ref, k_ref, vt_ref, g_ref, tab_ref, lam_ref, ng_ref,
         y_ref, qtb_ref, m_ref, acc_ref, s_ref, mt_ref) = refs
        e_ref = None
    else:
        (qt_ref, k_ref, e_ref, vt_ref, g_ref, tab_ref, kn_ref, nc_ref,
         y_ref, qtb_ref, m_ref, acc_ref, s_ref, mt_ref) = refs
    p = pl.program_id(1)
    qi = pl.program_id(2)

    qt = qt_ref[0].astype(F32)
    row = lax.broadcasted_iota(jnp.int32, (LANES, T), 0)
    q_lo = jnp.where(row < HALF, qt, 0.0)
    q_hi = jnp.where(row >= HALF, qt, 0.0)
    qtb_ref[:LANES, :] = jnp.concatenate([q_lo, q_hi], axis=1).astype(BF16)
    if not is_a:
        r2 = lax.broadcasted_iota(jnp.int32, (LANES, 2 * T), 0)
        c2 = lax.broadcasted_iota(jnp.int32, (LANES, 2 * T), 1)
        head = 2 * p + jnp.where(c2 >= T, 1, 0)
        sel = (r2 == head) | (r2 == head + B_HEADS) | (r2 == head + 2 * B_HEADS)
        qtb_ref[LANES:, :] = jnp.where(sel, 1.0, 0.0).astype(BF16)

    def scores(j):
        rows = pl.ds(pl.multiple_of(j * T, T), T)
        kt = k_ref[0, rows, :]
        if not is_a:
            kt = jnp.concatenate([kt, e_ref[0, rows, :]], axis=1)
        return jnp.dot(kt, qtb_ref[...], preferred_element_type=F32)

    def pv_dot(vts, pb):
        if is_a:
            return jnp.dot(vts[0], pb, preferred_element_type=F32)
        return jnp.concatenate(
            [jnp.dot(vts[0], pb[:, :T], preferred_element_type=F32),
             jnp.dot(vts[1], pb[:, T:], preferred_element_type=F32)], axis=1)

    def softmax_pv(vts, s, mt):
        m_old = m_ref[...]
        m_new = jnp.maximum(m_old, mt)
        alpha = jnp.exp2(m_old - m_new)
        pv = pv_dot(vts, jnp.exp2(s - m_new).astype(BF16))
        acc_ref[...] = alpha * acc_ref[...] + pv
        m_ref[...] = m_new

    def v_tile(j, valid=None):
        jc = jnp.maximum(j, 0)
        vts = [vt_ref[0, u, jc] for u in range(1 if is_a else 2)]
        if valid is not None:
            vts = [jnp.where(valid, vt, jnp.zeros_like(vt)) for vt in vts]
        return vts

    def scores_to(slot, j):
        s = scores(jnp.maximum(j, 0))
        s_ref[slot, :, :2 * T] = s
        mt_ref[slot] = jnp.max(s, axis=0, keepdims=True)

    nblk = T // BLK
    tab_sub = tab_ref[0, 0] if is_a else None
    tab_diag = tab_ref[0, 1]

    def diag_tile(s):
        pbs, mts = [], []
        for lb in range(2 * nblk):
            qb = lb % nblk
            lanes = slice(lb * BLK, (lb + 1) * BLK)
            parts = [s[:(qb - 1) * BLK, lanes]] if qb >= 2 else []
            if qb >= 1:
                below = s[(qb - 1) * BLK:qb * BLK, lanes]
                parts.append(below if tab_sub is None else below + tab_sub)
            parts.append(s[qb * BLK:(qb + 1) * BLK, lanes] + tab_diag)
            sb = parts[0] if len(parts) == 1 else jnp.concatenate(parts, axis=0)
            mt = jnp.max(sb, axis=0, keepdims=True)
            pb = jnp.exp2(sb - mt).astype(BF16)
            if qb < nblk - 1:
                pb = jnp.concatenate([pb, jnp.zeros((T - (qb + 1) * BLK, BLK), BF16)], axis=0)
            pbs.append(pb)
            mts.append(mt)
        acc_ref[...] = pv_dot(v_tile(qi), jnp.concatenate(pbs, axis=1))
        m_ref[...] = jnp.concatenate(mts, axis=1)

    s_diag = scores(qi)
    if is_a:
        s_near = scores(jnp.maximum(qi - 1, 0)) + jnp.where(qi >= 1, 0.0, NEG_INF)
        last = s_near[T - BLK:]
        last = jnp.concatenate(
            [last[:, :BLK] + tab_sub, last[:, BLK:T], last[:, T:T + BLK] + tab_sub,
             last[:, T + BLK:]], axis=1)
        s_near = jnp.concatenate([s_near[:T - BLK], last], axis=0)
        diag_tile(s_diag)
        top = qi - 2
        scores_to(0, top)
        softmax_pv(v_tile(qi - 1), s_near, jnp.max(s_near, axis=0, keepdims=True))
        n_far = jnp.maximum(qi - 1, 0)
    else:
        top = qi - 1
        scores_to(0, top)
        diag_tile(s_diag)
        need = None
        for half, qh in enumerate((q_lo, q_hi)):
            qn = jnp.sqrt(jnp.max(jnp.sum(qh * qh, axis=0, keepdims=True), axis=1, keepdims=True))
            m_min = jnp.min(m_ref[:, half * T:(half + 1) * T], axis=1, keepdims=True)
            bound = (SKIP_NORM_MARGIN * qn) * kn_ref[0, half] + SKIP_ABS_MARGIN
            nd = (nc_ref[0, half] + bound) > (m_min - SKIP_THRESH)
            need = nd if need is None else (need | nd)
        tiles = need.shape[0]
        jidx = lax.broadcasted_iota(jnp.int32, need.shape, 0)
        cand = jnp.where(need & (jidx < qi), jidx, qi)
        n_far = qi - jnp.min(cand.reshape(tiles * 8, LANES))

    def far_steps(t0, count):
        for u in range(count):
            t = t0 + u
            scores_to((u + 1) % 2, top - t - 1)
            valid = None if u % 2 == 0 else t < n_far
            softmax_pv(v_tile(top - t, valid), s_ref[u % 2, :, :2 * T], mt_ref[u % 2])

    done = 0
    left = n_far + (n_far & 1)
    for unroll in FAR_UNROLLS:
        trips = left // unroll

        def body(i, carry, unroll=unroll, done=done):
            far_steps(done + unroll * i, unroll)
            return carry

        lax.fori_loop(0, trips, body, 0)
        done = done + unroll * trips
        left = left - unroll * trips

    vd = acc_ref.shape[0] - ONES_ROWS
    o = acc_ref[:vd] * (1.0 / acc_ref[vd:vd + 1])
    if is_a:
        lam = lam_ref[0:1, 0:1]
        o2 = o[:, :T] - lam * o[:, T:]
        ms = jnp.mean(o2 * o2, axis=0, keepdims=True)
        o2 = o2 * lax.rsqrt(ms + LN_EPS) * ng_ref[...] * (1.0 - LAM_INIT)
    else:
        o2 = jnp.concatenate([o[:, :T], o[:, T:]], axis=0)
    y_ref[0] = (o2 * g_ref[0].astype(F32)).astype(BF16)


def _attn_call(qt, k, e, vt5, g, tab, lam, ng, kn, nc, *, is_a):
    b, d, s = qt.shape
    nq = s // T
    off = 0 if is_a else PAIRS
    kd = LANES if is_a else 2 * LANES
    v_rows = vt5.shape[3]
    in_specs = [
        pl.BlockSpec((1, LANES, T), lambda i, p, q: (i, p + off, q)),
        pl.BlockSpec((1, s, LANES), lambda i, p, q: (i, 0, p + off)),
    ]
    args = [qt, k]
    if not is_a:
        in_specs.append(pl.BlockSpec((1, s, LANES), lambda i, p, q: (i, 0, 0)))
        args.append(e)
    in_specs += [
        pl.BlockSpec((1, 1 if is_a else 2, s // TM, v_rows, TM), lambda i, p, q: (i, p, 0, 0, 0)),
        pl.BlockSpec((1, LANES, T), lambda i, p, q: (i, p + off, q)),
    ]
    args += [vt5, g]
    if is_a:
        in_specs.append(pl.BlockSpec((1, 2, BLK, BLK), lambda i, p, q: (p, 0, 0, 0)))
        in_specs.append(pl.BlockSpec((8, LANES), lambda i, p, q: (0, 0)))
        in_specs.append(pl.BlockSpec((LANES, 1), lambda i, p, q: (0, 0)))
        args += [tab, lam, ng]
    else:
        in_specs.append(pl.BlockSpec((1, 2, BLK, BLK), lambda i, p, q: (A_HEADS, 0, 0, 0)))
        stat_spec = pl.BlockSpec((1, 2, s // TM, 8, LANES), lambda i, p, q: (i, p, 0, 0, 0))
        in_specs += [stat_spec, stat_spec]
        args += [tab, kn, nc]
    return pl.pallas_call(
        functools.partial(_attn_kernel, is_a=is_a),
        grid=(b, PAIRS, nq),
        in_specs=in_specs,
        out_specs=pl.BlockSpec((1, LANES, T), lambda i, p, q: (i, p, q)),
        out_shape=jax.ShapeDtypeStruct((b, PAIRS * LANES, s), BF16),
        scratch_shapes=[
            pltpu.VMEM((kd, 2 * T), BF16),
            pltpu.VMEM((1, 2 * T), F32),
            pltpu.VMEM((v_rows, 2 * T), F32),
            pltpu.VMEM((2, T, 2 * T + LANES), F32),
            pltpu.VMEM((2, 1, 2 * T), F32),
        ],
        compiler_params=pltpu.CompilerParams(
            dimension_semantics=("arbitrary", "arbitrary", "arbitrary"),
            vmem_limit_bytes=VMEM_LIMIT),
        name="attn_a" if is_a else "attn_b",
    )(*args)


def _out_kernel(ya_ref, yb_ref, w1_ref, w2_ref, x_ref, gate_ref, lng_ref, lnb_ref, o_ref):
    tn = (((0,), (0,)), ((), ()))
    y = (lax.dot_general(ya_ref[0], w1_ref[...], tn, preferred_element_type=F32)
         + lax.dot_general(yb_ref[0], w2_ref[...], tn, preferred_element_type=F32))
    z = DEEPNORM_ALPHA * x_ref[0] + gate_ref[0] * y
    mu = jnp.mean(z, axis=-1, keepdims=True)
    zc = z - mu
    var = jnp.mean(zc * zc, axis=-1, keepdims=True)
    o_ref[0] = zc * lax.rsqrt(var + LN_EPS) * lng_ref[...] + lnb_ref[...]


def _out_call(ya, yb, w1, w2, x, gate, lng, lnb):
    b, s, d = x.shape
    half = ya.shape[1]
    return pl.pallas_call(
        _out_kernel,
        grid=(b, s // TO),
        in_specs=[
            pl.BlockSpec((1, half, TO), lambda i, t: (i, 0, t)),
            pl.BlockSpec((1, half, TO), lambda i, t: (i, 0, t)),
            pl.BlockSpec((half, d), lambda i, t: (0, 0)),
            pl.BlockSpec((half, d), lambda i, t: (0, 0)),
            pl.BlockSpec((1, TO, d), lambda i, t: (i, t, 0)),
            pl.BlockSpec((1, 1, d), lambda i, t: (i, 0, 0)),
            pl.BlockSpec((1, d), lambda i, t: (0, 0)),
            pl.BlockSpec((1, d), lambda i, t: (0, 0)),
        ],
        out_specs=pl.BlockSpec((1, TO, d), lambda i, t: (i, t, 0)),
        out_shape=jax.ShapeDtypeStruct((b, s, d), x.dtype),
        compiler_params=pltpu.CompilerParams(
            dimension_semantics=("arbitrary", "arbitrary"), vmem_limit_bytes=VMEM_LIMIT),
        name="out",
    )(ya, yb, w1, w2, x, gate, lng, lnb)


def kernel(x, c, w_in, w_out, rel_bias, lam_q1, lam_k1, lam_q2, lam_k2, diff_norm_g,
           b_forget, w_ada, b_ada, ln_g, ln_b):
    b, s, d = x.shape
    assert s % T == 0 and s % TO == 0 and T == TM and d == PAIRS * 2 * LANES
    layer = 0
    aw = PAIRS * LANES

    c8 = jnp.pad(c, ((0, 8 - b), (0, 0)))
    ada8, lam = _ada_call(c8, w_ada[layer], b_ada[layer][None], lam_q1[layer][None],
                          lam_k1[layer][None], lam_q2[layer][None], lam_k2[layer][None])
    ada = ada8[:b]
    shift = ada[:, None, :d]
    scale = ada[:, None, d:2 * d]
    gate = ada[:, None, 2 * d:]

    w = w_in[layer]
    grp = lambda i: w[:, i * aw:(i + 1) * aw]
    wbf = jnp.pad(w[:, 8 * aw:], ((0, 0), (0, B_HEADS)))
    wrow = jnp.concatenate([grp(1), grp(5)], axis=1).astype(BF16)
    wt = jnp.concatenate([grp(0), grp(4), grp(2), grp(6), grp(3), grp(7), wbf],
                         axis=1).T.astype(BF16)
    bfb = b_forget[layer].reshape(B_HEADS, 1).astype(F32)

    k, g, qt, vta, vtb, e, kn, nc = _proj_call(x, shift, scale, wrow, wt, bfb)
    tab = _table_call(rel_bias)
    ng = diff_norm_g[layer].reshape(LANES, 1)
    ya = _attn_call(qt, k, None, vta, g, tab, lam, ng, None, None, is_a=True)
    yb = _attn_call(qt, k, e, vtb, g, tab, None, None, kn, nc, is_a=False)

    wo = w_out[layer].astype(BF16)
    return _out_call(ya, yb, wo[:aw], wo[aw:], x, gate, ln_g[layer][None], ln_b[layer][None])
```

```python
import functools
import math

import jax
import jax.numpy as jnp
from jax import lax
from jax.experimental import pallas as pl
from jax.experimental.pallas import tpu as pltpu

F32 = jnp.float32
BF16 = jnp.bfloat16

LANES = 128
HALF = 64
A_HEADS = 4
B_HEADS = 8
PAIRS = 4
CHUNK = 64
N_BUCKETS = 32
MAX_DISTANCE = 128
LN_EPS = 1e-5
NEG_INF = -1e30
LOG2E = 1.4426950408889634
DEPTH = 1
DEEPNORM_ALPHA = (2 * DEPTH) ** 0.25
LAM_INIT = 0.8 - 0.6 * math.exp(-0.3 * 0)

T = 512
BLK = 128
TM = 512
TO = 2048
SKIP_THRESH = 152.0
SKIP_NORM_MARGIN = 1.02
SKIP_ABS_MARGIN = 0.5
FAR_UNROLLS = (8, 4, 2)
ONES_ROWS = 16
VA_ROWS = LANES + ONES_ROWS
VB_ROWS = HALF + ONES_ROWS
VMEM_LIMIT = 60 * 1024 * 1024


def _ada_kernel(c_ref, w_ref, b_ref, q1_ref, k1_ref, q2_ref, k2_ref, ada_ref, lam_ref):
    c = c_ref[...]
    sc = c * jax.nn.sigmoid(c)
    ada_ref[...] = jnp.dot(sc, w_ref[...], precision=lax.Precision.HIGHEST,
                           preferred_element_type=F32) + b_ref[...]
    s1 = jnp.sum(q1_ref[...] * k1_ref[...], axis=-1, keepdims=True)
    s2 = jnp.sum(q2_ref[...] * k2_ref[...], axis=-1, keepdims=True)
    lam = jnp.exp(s1) - jnp.exp(s2) + LAM_INIT
    lam_ref[...] = jnp.broadcast_to(lam, lam_ref.shape)


def _ada_call(c8, w_ada, b_ada, q1, k1, q2, k2):
    d3 = w_ada.shape[1]
    return pl.pallas_call(
        _ada_kernel,
        out_shape=(jax.ShapeDtypeStruct((c8.shape[0], d3), F32),
                   jax.ShapeDtypeStruct((8, LANES), F32)),
        compiler_params=pltpu.CompilerParams(vmem_limit_bytes=VMEM_LIMIT),
        name="ada",
    )(c8, w_ada, b_ada, q1, k1, q2, k2)


def _log_bucket_starts():
    nb = N_BUCKETS // 2
    e = nb // 2
    w = nb - e
    starts, n = [], e
    for j in range(1, w):
        while n ** w * e ** j < e ** w * MAX_DISTANCE ** j:
            n += 1
        starts.append(n)
    return starts


def _t5_bucket(rel):
    nb = N_BUCKETS // 2
    ret = jnp.where(rel > 0, nb, 0)
    n = jnp.abs(rel)
    max_exact = nb // 2
    large = jnp.full_like(n, max_exact)
    for start in _log_bucket_starts():
        large = large + jnp.where(n >= start, 1, 0)
    return ret + jnp.where(n < max_exact, n, large)


def _table_kernel(rb_ref, tab_ref):
    i = pl.program_id(0)
    h = jnp.minimum(i, A_HEADS - 1)
    kl = lax.broadcasted_iota(jnp.int32, (BLK, BLK), 0)
    ql = lax.broadcasted_iota(jnp.int32, (BLK, BLK), 1)
    far = rb_ref[N_BUCKETS // 2 - 1, h]
    for d in range(2):
        bucket = _t5_bucket(kl - ql + (d - 1) * BLK)
        bias = jnp.zeros((BLK, BLK), F32)
        for bk in range(N_BUCKETS):
            bias = jnp.where(bucket == bk, rb_ref[bk, h], bias)
        tab_a = (bias - far) * LOG2E
        if d == 1:
            tab_a = jnp.where((kl // CHUNK) <= (ql // CHUNK), tab_a, NEG_INF)
            tab_b = jnp.where(kl <= ql, 0.0, NEG_INF)
        else:
            tab_b = jnp.zeros((BLK, BLK), F32)
        tab_ref[0, d] = jnp.where(i < A_HEADS, tab_a, tab_b)


def _table_call(rel_bias):
    return pl.pallas_call(
        _table_kernel,
        grid=(A_HEADS + 1,),
        in_specs=[pl.BlockSpec(memory_space=pltpu.SMEM)],
        out_specs=pl.BlockSpec((1, 2, BLK, BLK), lambda i: (i, 0, 0, 0)),
        out_shape=jax.ShapeDtypeStruct((A_HEADS + 1, 2, BLK, BLK), F32),
        compiler_params=pltpu.CompilerParams(
            dimension_semantics=("arbitrary",), vmem_limit_bytes=VMEM_LIMIT),
        name="tables",
    )(rel_bias)


def _split3(v):
    hi = v.astype(BF16)
    r1 = v - hi.astype(F32)
    mid = r1.astype(BF16)
    lo = (r1 - mid.astype(F32)).astype(BF16)
    return hi, mid, lo


def _proj_kernel(x_ref, sh_ref, sc_ref, wrow_ref, wt_ref, bfb_ref,
                 k_ref, g_ref, qt_ref, vta_ref, vtb_ref, e_ref, kn_ref, nc_ref, carry_ref):
    t = pl.program_id(1)
    d = x_ref.shape[2]
    x = x_ref[0]
    mu = jnp.mean(x, axis=-1, keepdims=True)
    xc = x - mu
    var = jnp.mean(xc * xc, axis=-1, keepdims=True)
    h = (xc * lax.rsqrt(var + LN_EPS)) * (1.0 + sc_ref[0]) + sh_ref[0]
    hb = h.astype(BF16)

    kb = jnp.dot(hb, wrow_ref[...], preferred_element_type=F32).astype(BF16)
    k_ref[0] = kb

    tt = lax.dot_general(wt_ref[...], hb, (((1,), (1,)), ((), ())),
                         preferred_element_type=F32)
    qt_ref[0] = (tt[:d] * (LOG2E * HALF ** -0.5)).astype(BF16)
    g = tt[2 * d:3 * d]
    g_ref[0] = (g * jax.nn.sigmoid(g)).astype(BF16)

    z = tt[3 * d:3 * d + B_HEADS] + bfb_ref[...]
    logf = jnp.minimum(z, 0.0) - jnp.log1p(jnp.exp(-jnp.abs(z)))
    ri = lax.broadcasted_iota(jnp.int32, (TM, TM), 0)
    ci = lax.broadcasted_iota(jnp.int32, (TM, TM), 1)
    upper = jnp.where(ri <= ci, 1.0, 0.0).astype(BF16)
    hi, mid, lo = _split3(logf)
    pieces = jnp.concatenate([hi.astype(F32), mid.astype(F32), lo.astype(F32),
                              jnp.zeros((B_HEADS, TM), F32)], axis=0)
    cs = jnp.dot(pieces.astype(BF16), upper, preferred_element_type=F32)

    @pl.when(t == 0)
    def _():
        carry_ref[...] = jnp.zeros_like(carry_ref)

    carry = jnp.concatenate([carry_ref[...]] * (TM // LANES), axis=1)
    cf = (cs[:B_HEADS] + cs[B_HEADS:2 * B_HEADS]) + cs[2 * B_HEADS:3 * B_HEADS] + carry
    carry_ref[...] = jnp.broadcast_to(cf[:, TM - 1:TM], carry_ref.shape)
    ncf = -cf * LOG2E
    vh, vm, vl = _split3(ncf)
    pt = jnp.concatenate([vh.astype(F32), vm.astype(F32), vl.astype(F32),
                          jnp.zeros((LANES - 3 * B_HEADS, TM), F32)], axis=0)
    e_ref[0] = pt.T.astype(BF16)

    kf = kb[:, d // 2:].astype(F32)
    gc = lax.broadcasted_iota(jnp.int32, (d // 2, LANES), 0)
    gh = lax.broadcasted_iota(jnp.int32, (d // 2, LANES), 1)
    gsel = jnp.where(gc // HALF == gh, 1.0, 0.0).astype(BF16)
    ksq = jnp.dot((kf * kf).astype(BF16), gsel, preferred_element_type=F32)
    knt = jnp.sqrt(jnp.max(ksq, axis=0, keepdims=True))
    nct = jnp.max(ncf, axis=1, keepdims=True)
    for hh in range(B_HEADS):
        kn_ref[0, hh, 0] = jnp.broadcast_to(knt[:, hh:hh + 1], (8, LANES))
        nc_ref[0, hh, 0] = jnp.broadcast_to(nct[hh:hh + 1, :], (8, LANES))

    hd = d // 2
    vta_ref[0, :, 0, :LANES, :] = tt[d:d + hd].reshape(A_HEADS, LANES, TM).astype(BF16)
    vtb_ref[0, :, 0, :HALF, :] = tt[d + hd:2 * d].reshape(B_HEADS, HALF, TM).astype(BF16)
    vta_ref[0, :, 0, LANES:, :] = jnp.ones((A_HEADS, ONES_ROWS, TM), BF16)
    vtb_ref[0, :, 0, HALF:, :] = jnp.ones((B_HEADS, ONES_ROWS, TM), BF16)


def _proj_call(x, shift, scale, wrow, wt, bfb):
    b, s, d = x.shape
    nt = s // TM
    nrow = wrow.shape[1]
    return pl.pallas_call(
        _proj_kernel,
        grid=(b, nt),
        in_specs=[
            pl.BlockSpec((1, TM, d), lambda i, t: (i, t, 0)),
            pl.BlockSpec((1, 1, d), lambda i, t: (i, 0, 0)),
            pl.BlockSpec((1, 1, d), lambda i, t: (i, 0, 0)),
            pl.BlockSpec((d, nrow), lambda i, t: (0, 0)),
            pl.BlockSpec((wt.shape[0], d), lambda i, t: (0, 0)),
            pl.BlockSpec((B_HEADS, 1), lambda i, t: (0, 0)),
        ],
        out_specs=(
            pl.BlockSpec((1, TM, d), lambda i, t: (i, t, 0)),
            pl.BlockSpec((1, d, TM), lambda i, t: (i, 0, t)),
            pl.BlockSpec((1, d, TM), lambda i, t: (i, 0, t)),
            pl.BlockSpec((1, A_HEADS, 1, VA_ROWS, TM), lambda i, t: (i, 0, t, 0, 0)),
            pl.BlockSpec((1, B_HEADS, 1, VB_ROWS, TM), lambda i, t: (i, 0, t, 0, 0)),
            pl.BlockSpec((1, TM, LANES), lambda i, t: (i, t, 0)),
            pl.BlockSpec((1, B_HEADS, 1, 8, LANES), lambda i, t: (i, 0, t, 0, 0)),
            pl.BlockSpec((1, B_HEADS, 1, 8, LANES), lambda i, t: (i, 0, t, 0, 0)),
        ),
        out_shape=(
            jax.ShapeDtypeStruct((b, s, d), BF16),
            jax.ShapeDtypeStruct((b, d, s), BF16),
            jax.ShapeDtypeStruct((b, d, s), BF16),
            jax.ShapeDtypeStruct((b, A_HEADS, nt, VA_ROWS, TM), BF16),
            jax.ShapeDtypeStruct((b, B_HEADS, nt, VB_ROWS, TM), BF16),
            jax.ShapeDtypeStruct((b, s, LANES), BF16),
            jax.ShapeDtypeStruct((b, B_HEADS, nt, 8, LANES), F32),
            jax.ShapeDtypeStruct((b, B_HEADS, nt, 8, LANES), F32),
        ),
        scratch_shapes=[pltpu.VMEM((B_HEADS, LANES), F32)],
        compiler_params=pltpu.CompilerParams(
            dimension_semantics=("arbitrary", "arbitrary"), vmem_limit_bytes=VMEM_LIMIT),
        name="proj",
    )(x, shift, scale, wrow, wt, bfb)


def _attn_kernel(*refs, is_a):
    if is_a:
        (qt_ref, k_ref, vt_ref, g_ref, tab_ref, lam_ref, ng_ref,
         y_ref, qtb_ref, m_ref, s_ref, mt_ref, acc_ref) = refs
        e_ref = None
    else:
        (qt_ref, k_ref, e_ref, vt_ref, g_ref, tab_ref, kn_ref, nc_ref,
         y_ref, qtb_ref, m_ref, s_ref, mt_ref, acc_ref) = refs
    p = pl.program_id(1)
    qi = pl.program_id(2)

    qt = qt_ref[0].astype(F32)
    row = lax.broadcasted_iota(jnp.int32, (LANES, T), 0)
    q_lo = jnp.where(row < HALF, qt, 0.0)
    q_hi = jnp.where(row >= HALF, qt, 0.0)
    qtb_ref[:LANES, :] = jnp.concatenate([q_lo, q_hi], axis=1).astype(BF16)
    if not is_a:
        r2 = lax.broadcasted_iota(jnp.int32, (LANES, 2 * T), 0)
        c2 = lax.broadcasted_iota(jnp.int32, (LANES, 2 * T), 1)
        head = 2 * p + jnp.where(c2 >= T, 1, 0)
        sel = (r2 == head) | (r2 == head + B_HEADS) | (r2 == head + 2 * B_HEADS)
        qtb_ref[LANES:, :] = jnp.where(sel, 1.0, 0.0).astype(BF16)

    def scores(j):
        rows = pl.ds(pl.multiple_of(j * T, T), T)
        kt = k_ref[0, rows, :]
        if not is_a:
            kt = jnp.concatenate([kt, e_ref[0, rows, :]], axis=1)
        return jnp.dot(kt, qtb_ref[...], preferred_element_type=F32)

    def pv_dot(vts, pb):
        if is_a:
            return jnp.dot(vts[0], pb, preferred_element_type=F32)
        return jnp.concatenate(
            [jnp.dot(vts[0], pb[:, :T], preferred_element_type=F32),
             jnp.dot(vts[1], pb[:, T:], preferred_element_type=F32)], axis=1)

    def softmax_pv(vts, s, mt):
        m_old = m_ref[...]
        m_new = jnp.maximum(m_old, mt)
        alpha = jnp.exp2(m_old - m_new)
        pv = pv_dot(vts, jnp.exp2(s - m_new).astype(BF16))
        acc_ref[:, :2 * T] = alpha * acc_ref[:, :2 * T] + pv
        m_ref[...] = m_new

    def v_tile(j, valid=None):
        jc = jnp.maximum(j, 0)
        vts = [vt_ref[0, u, jc] for u in range(1 if is_a else 2)]
        if valid is not None:
            vts = [jnp.where(valid, vt, jnp.zeros_like(vt)) for vt in vts]
        return vts

    def scores_to(slot, j):
        s = scores(jnp.maximum(j, 0))
        s_ref[slot, :, :2 * T] = s
        mt_ref[slot] = jnp.max(s, axis=0, keepdims=True)

    nblk = T // BLK
    tab_sub = tab_ref[0, 0] if is_a else None
    tab_diag = tab_ref[0, 1]

    def diag_tile(s):
        pbs, mts = [], []
        for lb in range(2 * nblk):
            qb = lb % nblk
            lanes = slice(lb * BLK, (lb + 1) * BLK)
            parts = [s[:(qb - 1) * BLK, lanes]] if qb >= 2 else []
            if qb >= 1:
                below = s[(qb - 1) * BLK:qb * BLK, lanes]
                parts.append(below if tab_sub is None else below + tab_sub)
            parts.append(s[qb * BLK:(qb + 1) * BLK, lanes] + tab_diag)
            sb = parts[0] if len(parts) == 1 else jnp.concatenate(parts, axis=0)
            mt = jnp.max(sb, axis=0, keepdims=True)
            pb = jnp.exp2(sb - mt).astype(BF16)
            if qb < nblk - 1:
                pb = jnp.concatenate([pb, jnp.zeros((T - (qb + 1) * BLK, BLK), BF16)], axis=0)
            pbs.append(pb)
            mts.append(mt)
        acc_ref[:, :2 * T] = pv_dot(v_tile(qi), jnp.concatenate(pbs, axis=1))
        m_ref[...] = jnp.concatenate(mts, axis=1)

    s_diag = scores(qi)
    if is_a:
        s_near = scores(jnp.maximum(qi - 1, 0)) + jnp.where(qi >= 1, 0.0, NEG_INF)
        last = s_near[T - BLK:]
        last = jnp.concatenate(
            [last[:, :BLK] + tab_sub, last[:, BLK:T], last[:, T:T + BLK] + tab_sub,
             last[:, T + BLK:]], axis=1)
        s_near = jnp.concatenate([s_near[:T - BLK], last], axis=0)
        diag_tile(s_diag)
        top = qi - 2
        scores_to(0, top)
        softmax_pv(v_tile(qi - 1), s_near, jnp.max(s_near, axis=0, keepdims=True))
        n_far = jnp.maximum(qi - 1, 0)
    else:
        top = qi - 1
        scores_to(0, top)
        diag_tile(s_diag)
        need = None
        for half, qh in enumerate((q_lo, q_hi)):
            qn = jnp.sqrt(jnp.max(jnp.sum(qh * qh, axis=0, keepdims=True), axis=1, keepdims=True))
            m_min = jnp.min(m_ref[:, half * T:(half + 1) * T], axis=1, keepdims=True)
            bound = (SKIP_NORM_MARGIN * qn) * kn_ref[0, half] + SKIP_ABS_MARGIN
            nd = (nc_ref[0, half] + bound) > (m_min - SKIP_THRESH)
            need = nd if need is None else (need | nd)
        tiles = need.shape[0]
        jidx = lax.broadcasted_iota(jnp.int32, need.shape, 0)
        cand = jnp.where(need & (jidx < qi), jidx, qi)
        n_far = qi - jnp.min(cand.reshape(tiles * 8, LANES))

    def far_steps(t0, count):
        for u in range(count):
            t = t0 + u
            scores_to((u + 1) % 2, top - t - 1)
            valid = None if u % 2 == 0 else t < n_far
            softmax_pv(v_tile(top - t, valid), s_ref[u % 2, :, :2 * T], mt_ref[u % 2])

    done = 0
    left = n_far + (n_far & 1)
    for unroll in FAR_UNROLLS:
        trips = left // unroll

        def body(i, carry, unroll=unroll, done=done):
            far_steps(done + unroll * i, unroll)
            return carry

        lax.fori_loop(0, trips, body, 0)
        done = done + unroll * trips
        left = left - unroll * trips

    vd = acc_ref.shape[0] - ONES_ROWS
    o = acc_ref[:vd, :2 * T] * (1.0 / acc_ref[vd:vd + 1, :2 * T])
    if is_a:
        lam = lam_ref[0:1, 0:1]
        o2 = o[:, :T] - lam * o[:, T:]
        ms = jnp.mean(o2 * o2, axis=0, keepdims=True)
        o2 = o2 * lax.rsqrt(ms + LN_EPS) * ng_ref[...] * (1.0 - LAM_INIT)
    else:
        o2 = jnp.concatenate([o[:, :T], o[:, T:]], axis=0)
    y_ref[0] = (o2 * g_ref[0].astype(F32)).astype(BF16)


def _attn_call(qt, k, e, vt5, g, tab, lam, ng, kn, nc, *, is_a):
    b, d, s = qt.shape
    nq = s // T
    off = 0 if is_a else PAIRS
    kd = LANES if is_a else 2 * LANES
    v_rows = vt5.shape[3]
    in_specs = [
        pl.BlockSpec((1, LANES, T), lambda i, p, q: (i, p + off, q)),
        pl.BlockSpec((1, s, LANES), lambda i, p, q: (i, 0, p + off)),
    ]
    args = [qt, k]
    if not is_a:
        in_specs.append(pl.BlockSpec((1, s, LANES), lambda i, p, q: (i, 0, 0)))
        args.append(e)
    in_specs += [
        pl.BlockSpec((1, 1 if is_a else 2, s // TM, v_rows, TM), lambda i, p, q: (i, p, 0, 0, 0)),
        pl.BlockSpec((1, LANES, T), lambda i, p, q: (i, p + off, q)),
    ]
    args += [vt5, g]
    if is_a:
        in_specs.append(pl.BlockSpec((1, 2, BLK, BLK), lambda i, p, q: (p, 0, 0, 0)))
        in_specs.append(pl.BlockSpec((8, LANES), lambda i, p, q: (0, 0)))
        in_specs.append(pl.BlockSpec((LANES, 1), lambda i, p, q: (0, 0)))
        args += [tab, lam, ng]
    else:
        in_specs.append(pl.BlockSpec((1, 2, BLK, BLK), lambda i, p, q: (A_HEADS, 0, 0, 0)))
        stat_spec = pl.BlockSpec((1, 2, s // TM, 8, LANES), lambda i, p, q: (i, p, 0, 0, 0))
        in_specs += [stat_spec, stat_spec]
        args += [tab, kn, nc]
    return pl.pallas_call(
        functools.partial(_attn_kernel, is_a=is_a),
        grid=(b, PAIRS, nq),
        in_specs=in_specs,
        out_specs=pl.BlockSpec((1, LANES, T), lambda i, p, q: (i, p, q)),
        out_shape=jax.ShapeDtypeStruct((b, PAIRS * LANES, s), BF16),
        scratch_shapes=[
            pltpu.VMEM((kd, 2 * T), BF16),
            pltpu.VMEM((1, 2 * T), F32),
            pltpu.VMEM((2, T, 2 * T + LANES), F32),
            pltpu.VMEM((2, 1, 2 * T), F32),
            pltpu.VMEM((v_rows, 2 * T + LANES), F32),
        ],
        compiler_params=pltpu.CompilerParams(
            dimension_semantics=("arbitrary", "arbitrary", "arbitrary"),
            vmem_limit_bytes=VMEM_LIMIT),
        name="attn_a" if is_a else "attn_b",
    )(*args)


def _out_kernel(ya_ref, yb_ref, w1_ref, w2_ref, x_ref, gate_ref, lng_ref, lnb_ref, o_ref):
    tn = (((0,), (0,)), ((), ()))
    for r0 in range(0, TO, TM):
        rows = slice(r0, r0 + TM)
        y = (lax.dot_general(ya_ref[0, :, rows], w1_ref[...], tn, preferred_element_type=F32)
             + lax.dot_general(yb_ref[0, :, rows], w2_ref[...], tn, preferred_element_type=F32))
        z = DEEPNORM_ALPHA * x_ref[0, rows, :] + gate_ref[0] * y
        mu = jnp.mean(z, axis=-1, keepdims=True)
        zc = z - mu
        var = jnp.mean(zc * zc, axis=-1, keepdims=True)
        o_ref[0, rows, :] = zc * lax.rsqrt(var + LN_EPS) * lng_ref[...] + lnb_ref[...]


def _out_call(ya, yb, w1, w2, x, gate, lng, lnb):
    b, s, d = x.shape
    half = ya.shape[1]
    return pl.pallas_call(
        _out_kernel,
        grid=(b, s // TO),
        in_specs=[
            pl.BlockSpec((1, half, TO), lambda i, t: (i, 0, t)),
            pl.BlockSpec((1, half, TO), lambda i, t: (i, 0, t)),
            pl.BlockSpec((half, d), lambda i, t: (0, 0)),
            pl.BlockSpec((half, d), lambda i, t: (0, 0)),
            pl.BlockSpec((1, TO, d), lambda i, t: (i, t, 0)),
            pl.BlockSpec((1, 1, d), lambda i, t: (i, 0, 0)),
            pl.BlockSpec((1, d), lambda i, t: (0, 0)),
            pl.BlockSpec((1, d), lambda i, t: (0, 0)),
        ],
        out_specs=pl.BlockSpec((1, TO, d), lambda i, t: (i, t, 0)),
        out_shape=jax.ShapeDtypeStruct((b, s, d), x.dtype),
        compiler_params=pltpu.CompilerParams(
            dimension_semantics=("arbitrary", "arbitrary"), vmem_limit_bytes=VMEM_LIMIT),
        name="out",
    )(ya, yb, w1, w2, x, gate, lng, lnb)


def kernel(x, c, w_in, w_out, rel_bias, lam_q1, lam_k1, lam_q2, lam_k2, diff_norm_g,
           b_forget, w_ada, b_ada, ln_g, ln_b):
    b, s, d = x.shape
    assert s % T == 0 and s % TO == 0 and T == TM and d == PAIRS * 2 * LANES
    layer = 0
    aw = PAIRS * LANES

    c8 = jnp.pad(c, ((0, 8 - b), (0, 0)))
    ada8, lam = _ada_call(c8, w_ada[layer], b_ada[layer][None], lam_q1[layer][None],
                          lam_k1[layer][None], lam_q2[layer][None], lam_k2[layer][None])
    ada = ada8[:b]
    shift = ada[:, None, :d]
    scale = ada[:, None, d:2 * d]
    gate = ada[:, None, 2 * d:]

    w = w_in[layer]
    grp = lambda i: w[:, i * aw:(i + 1) * aw]
    wbf = jnp.pad(w[:, 8 * aw:], ((0, 0), (0, B_HEADS)))
    wrow = jnp.concatenate([grp(1), grp(5)], axis=1).astype(BF16)
    wt = jnp.concatenate([grp(0), grp(4), grp(2), grp(6), grp(3), grp(7), wbf],
                         axis=1).T.astype(BF16)
    bfb = b_forget[layer].reshape(B_HEADS, 1).astype(F32)

    k, g, qt, vta, vtb, e, kn, nc = _proj_call(x, shift, scale, wrow, wt, bfb)
    tab = _table_call(rel_bias)
    ng = diff_norm_g[layer].reshape(LANES, 1)
    ya = _attn_call(qt, k, None, vta, g, tab, lam, ng, None, None, is_a=True)
    yb = _attn_call(qt, k, e, vtb, g, tab, None, None, kn, nc, is_a=False)

    wo = w_out[layer].astype(BF16)
    return _out_call(ya, yb, wo[:aw], wo[aw:], x, gate, ln_g[layer][None], ln_b[layer][None])
```

```python
import functools
import math

import jax
import jax.numpy as jnp
from jax import lax
from jax.experimental import pallas as pl
from jax.experimental.pallas import tpu as pltpu

F32 = jnp.float32
BF16 = jnp.bfloat16

LANES = 128
HALF = 64
A_HEADS = 4
B_HEADS = 8
PAIRS = 4
CHUNK = 64
N_BUCKETS = 32
MAX_DISTANCE = 128
LN_EPS = 1e-5
NEG_INF = -1e30
LOG2E = 1.4426950408889634
DEPTH = 1
DEEPNORM_ALPHA = (2 * DEPTH) ** 0.25
LAM_INIT = 0.8 - 0.6 * math.exp(-0.3 * 0)

T = 512
BLK = 128
TM = 512
TO = 2048
SKIP_THRESH = 152.0
SKIP_NORM_MARGIN = 1.02
SKIP_ABS_MARGIN = 0.5
FAR_UNROLLS = (8, 4, 2)
ONES_ROWS = 16
VA_ROWS = LANES + ONES_ROWS
VB_ROWS = HALF + ONES_ROWS
VMEM_LIMIT = 60 * 1024 * 1024


def _ada_kernel(c_ref, w_ref, b_ref, q1_ref, k1_ref, q2_ref, k2_ref, ada_ref, lam_ref):
    c = c_ref[...]
    sc = c * jax.nn.sigmoid(c)
    ada_ref[...] = jnp.dot(sc, w_ref[...], precision=lax.Precision.HIGHEST,
                           preferred_element_type=F32) + b_ref[...]
    s1 = jnp.sum(q1_ref[...] * k1_ref[...], axis=-1, keepdims=True)
    s2 = jnp.sum(q2_ref[...] * k2_ref[...], axis=-1, keepdims=True)
    lam = jnp.exp(s1) - jnp.exp(s2) + LAM_INIT
    lam_ref[...] = jnp.broadcast_to(lam, lam_ref.shape)


def _ada_call(c8, w_ada, b_ada, q1, k1, q2, k2):
    d3 = w_ada.shape[1]
    return pl.pallas_call(
        _ada_kernel,
        out_shape=(jax.ShapeDtypeStruct((c8.shape[0], d3), F32),
                   jax.ShapeDtypeStruct((8, LANES), F32)),
        compiler_params=pltpu.CompilerParams(vmem_limit_bytes=VMEM_LIMIT),
        name="ada",
    )(c8, w_ada, b_ada, q1, k1, q2, k2)


def _log_bucket_starts():
    nb = N_BUCKETS // 2
    e = nb // 2
    w = nb - e
    starts, n = [], e
    for j in range(1, w):
        while n ** w * e ** j < e ** w * MAX_DISTANCE ** j:
            n += 1
        starts.append(n)
    return starts


def _t5_bucket(rel):
    nb = N_BUCKETS // 2
    ret = jnp.where(rel > 0, nb, 0)
    n = jnp.abs(rel)
    max_exact = nb // 2
    large = jnp.full_like(n, max_exact)
    for start in _log_bucket_starts():
        large = large + jnp.where(n >= start, 1, 0)
    return ret + jnp.where(n < max_exact, n, large)


def _table_kernel(rb_ref, tab_ref):
    i = pl.program_id(0)
    h = jnp.minimum(i, A_HEADS - 1)
    kl = lax.broadcasted_iota(jnp.int32, (BLK, BLK), 0)
    ql = lax.broadcasted_iota(jnp.int32, (BLK, BLK), 1)
    far = rb_ref[N_BUCKETS // 2 - 1, h]
    for d in range(2):
        bucket = _t5_bucket(kl - ql + (d - 1) * BLK)
        bias = jnp.zeros((BLK, BLK), F32)
        for bk in range(N_BUCKETS):
            bias = jnp.where(bucket == bk, rb_ref[bk, h], bias)
        tab_a = (bias - far) * LOG2E
        if d == 1:
            tab_a = jnp.where((kl // CHUNK) <= (ql // CHUNK), tab_a, NEG_INF)
            tab_b = jnp.where(kl <= ql, 0.0, NEG_INF)
        else:
            tab_b = jnp.zeros((BLK, BLK), F32)
        tab_ref[0, d] = jnp.where(i < A_HEADS, tab_a, tab_b)


def _table_call(rel_bias):
    return pl.pallas_call(
        _table_kernel,
        grid=(A_HEADS + 1,),
        in_specs=[pl.BlockSpec(memory_space=pltpu.SMEM)],
        out_specs=pl.BlockSpec((1, 2, BLK, BLK), lambda i: (i, 0, 0, 0)),
        out_shape=jax.ShapeDtypeStruct((A_HEADS + 1, 2, BLK, BLK), F32),
        compiler_params=pltpu.CompilerParams(
            dimension_semantics=("arbitrary",), vmem_limit_bytes=VMEM_LIMIT),
        name="tables",
    )(rel_bias)


def _split3(v):
    hi = v.astype(BF16)
    r1 = v - hi.astype(F32)
    mid = r1.astype(BF16)
    lo = (r1 - mid.astype(F32)).astype(BF16)
    return hi, mid, lo


def _proj_kernel(x_ref, sh_ref, sc_ref, wrow_ref, wt_ref, bfb_ref,
                 k_ref, g_ref, qt_ref, vta_ref, vtb_ref, e_ref, kn_ref, nc_ref, carry_ref):
    t = pl.program_id(1)
    d = x_ref.shape[2]
    x = x_ref[0]
    mu = jnp.mean(x, axis=-1, keepdims=True)
    xc = x - mu
    var = jnp.mean(xc * xc, axis=-1, keepdims=True)
    h = (xc * lax.rsqrt(var + LN_EPS)) * (1.0 + sc_ref[0]) + sh_ref[0]
    hb = h.astype(BF16)

    kb = jnp.dot(hb, wrow_ref[...], preferred_element_type=F32).astype(BF16)
    k_ref[0] = kb

    tt = lax.dot_general(wt_ref[...], hb, (((1,), (1,)), ((), ())),
                         preferred_element_type=F32)
    qt_ref[0] = (tt[:d] * (LOG2E * HALF ** -0.5)).astype(BF16)
    g = tt[2 * d:3 * d]
    g_ref[0] = (g * jax.nn.sigmoid(g)).astype(BF16)

    z = tt[3 * d:3 * d + B_HEADS] + bfb_ref[...]
    logf = jnp.minimum(z, 0.0) - jnp.log1p(jnp.exp(-jnp.abs(z)))
    ri = lax.broadcasted_iota(jnp.int32, (TM, TM), 0)
    ci = lax.broadcasted_iota(jnp.int32, (TM, TM), 1)
    upper = jnp.where(ri <= ci, 1.0, 0.0).astype(BF16)
    hi, mid, lo = _split3(logf)
    pieces = jnp.concatenate([hi.astype(F32), mid.astype(F32), lo.astype(F32),
                              jnp.zeros((B_HEADS, TM), F32)], axis=0)
    cs = jnp.dot(pieces.astype(BF16), upper, preferred_element_type=F32)

    @pl.when(t == 0)
    def _():
        carry_ref[...] = jnp.zeros_like(carry_ref)

    carry = jnp.concatenate([carry_ref[...]] * (TM // LANES), axis=1)
    cf = (cs[:B_HEADS] + cs[B_HEADS:2 * B_HEADS]) + cs[2 * B_HEADS:3 * B_HEADS] + carry
    carry_ref[...] = jnp.broadcast_to(cf[:, TM - 1:TM], carry_ref.shape)
    ncf = -cf * LOG2E
    vh, vm, vl = _split3(ncf)
    pt = jnp.concatenate([vh.astype(F32), vm.astype(F32), vl.astype(F32),
                          jnp.zeros((LANES - 3 * B_HEADS, TM), F32)], axis=0)
    e_ref[0] = pt.T.astype(BF16)

    kf = kb[:, d // 2:].astype(F32)
    gc = lax.broadcasted_iota(jnp.int32, (d // 2, LANES), 0)
    gh = lax.broadcasted_iota(jnp.int32, (d // 2, LANES), 1)
    gsel = jnp.where(gc // HALF == gh, 1.0, 0.0).astype(BF16)
    ksq = jnp.dot((kf * kf).astype(BF16), gsel, preferred_element_type=F32)
    knt = jnp.sqrt(jnp.max(ksq, axis=0, keepdims=True))
    nct = jnp.max(ncf, axis=1, keepdims=True)
    for hh in range(B_HEADS):
        kn_ref[0, hh, 0] = jnp.broadcast_to(knt[:, hh:hh + 1], (8, LANES))
        nc_ref[0, hh, 0] = jnp.broadcast_to(nct[hh:hh + 1, :], (8, LANES))

    hd = d // 2
    vta_ref[0, :, 0, :LANES, :] = tt[d:d + hd].reshape(A_HEADS, LANES, TM).astype(BF16)
    vtb_ref[0, :, 0, :HALF, :] = tt[d + hd:2 * d].reshape(B_HEADS, HALF, TM).astype(BF16)
    vta_ref[0, :, 0, LANES:, :] = jnp.ones((A_HEADS, ONES_ROWS, TM), BF16)
    vtb_ref[0, :, 0, HALF:, :] = jnp.ones((B_HEADS, ONES_ROWS, TM), BF16)


def _proj_call(x, shift, scale, wrow, wt, bfb):
    b, s, d = x.shape
    nt = s // TM
    nrow = wrow.shape[1]
    return pl.pallas_call(
        _proj_kernel,
        grid=(b, nt),
        in_specs=[
            pl.BlockSpec((1, TM, d), lambda i, t: (i, t, 0)),
            pl.BlockSpec((1, 1, d), lambda i, t: (i, 0, 0)),
            pl.BlockSpec((1, 1, d), lambda i, t: (i, 0, 0)),
            pl.BlockSpec((d, nrow), lambda i, t: (0, 0)),
            pl.BlockSpec((wt.shape[0], d), lambda i, t: (0, 0)),
            pl.BlockSpec((B_HEADS, 1), lambda i, t: (0, 0)),
        ],
        out_specs=(
            pl.BlockSpec((1, TM, d), lambda i, t: (i, t, 0)),
            pl.BlockSpec((1, d, TM), lambda i, t: (i, 0, t)),
            pl.BlockSpec((1, d, TM), lambda i, t: (i, 0, t)),
            pl.BlockSpec((1, A_HEADS, 1, VA_ROWS, TM), lambda i, t: (i, 0, t, 0, 0)),
            pl.BlockSpec((1, B_HEADS, 1, VB_ROWS, TM), lambda i, t: (i, 0, t, 0, 0)),
            pl.BlockSpec((1, TM, LANES), lambda i, t: (i, t, 0)),
            pl.BlockSpec((1, B_HEADS, 1, 8, LANES), lambda i, t: (i, 0, t, 0, 0)),
            pl.BlockSpec((1, B_HEADS, 1, 8, LANES), lambda i, t: (i, 0, t, 0, 0)),
        ),
        out_shape=(
            jax.ShapeDtypeStruct((b, s, d), BF16),
            jax.ShapeDtypeStruct((b, d, s), BF16),
            jax.ShapeDtypeStruct((b, d, s), BF16),
            jax.ShapeDtypeStruct((b, A_HEADS, nt, VA_ROWS, TM), BF16),
            jax.ShapeDtypeStruct((b, B_HEADS, nt, VB_ROWS, TM), BF16),
            jax.ShapeDtypeStruct((b, s, LANES), BF16),
            jax.ShapeDtypeStruct((b, B_HEADS, nt, 8, LANES), F32),
            jax.ShapeDtypeStruct((b, B_HEADS, nt, 8, LANES), F32),
        ),
        scratch_shapes=[pltpu.VMEM((B_HEADS, LANES), F32)],
        compiler_params=pltpu.CompilerParams(
            dimension_semantics=("arbitrary", "arbitrary"), vmem_limit_bytes=VMEM_LIMIT),
        name="proj",
    )(x, shift, scale, wrow, wt, bfb)


def _attn_kernel(*refs, is_a):
    if is_a:
        (qt_ref, k_ref, vt_ref, g_ref, tab_ref, lam_ref, ng_ref,
         y_ref, qtb_ref, m_ref, acc_ref, s_ref, mt_ref) = refs
        e_ref = None
    else:
        (qt_ref, k_ref, e_ref, vt_ref, g_ref, tab_ref, kn_ref, nc_ref,
         y_ref, qtb_ref, m_ref, acc_ref, s_ref, mt_ref) = refs
    p = pl.program_id(1)
    qi = pl.program_id(2)

    qt = qt_ref[0].astype(F32)
    row = lax.broadcasted_iota(jnp.int32, (LANES, T), 0)
    q_lo = jnp.where(row < HALF, qt, 0.0)
    q_hi = jnp.where(row >= HALF, qt, 0.0)
    qtb_ref[:LANES, :] = jnp.concatenate([q_lo, q_hi], axis=1).astype(BF16)
    if not is_a:
        r2 = lax.broadcasted_iota(jnp.int32, (LANES, 2 * T), 0)
        c2 = lax.broadcasted_iota(jnp.int32, (LANES, 2 * T), 1)
        head = 2 * p + jnp.where(c2 >= T, 1, 0)
        sel = (r2 == head) | (r2 == head + B_HEADS) | (r2 == head + 2 * B_HEADS)
        qtb_ref[LANES:, :] = jnp.where(sel, 1.0, 0.0).astype(BF16)

    def scores(j):
        rows = pl.ds(pl.multiple_of(j * T, T), T)
        kt = k_ref[0, rows, :]
        if not is_a:
            kt = jnp.concatenate([kt, e_ref[0, rows, :]], axis=1)
        return jnp.dot(kt, qtb_ref[...], preferred_element_type=F32)

    def pv_dot(vts, pb):
        if is_a:
            return jnp.dot(vts[0], pb, preferred_element_type=F32)
        return jnp.concatenate(
            [jnp.dot(vts[0], pb[:, :T], preferred_element_type=F32),
             jnp.dot(vts[1], pb[:, T:], preferred_element_type=F32)], axis=1)

    def softmax_pv(vts, s, mt):
        m_old = m_ref[...]
        m_new = jnp.maximum(m_old, mt)
        alpha = jnp.exp2(m_old - m_new)
        pv = pv_dot(vts, jnp.exp2(s - m_new).astype(BF16))
        acc_ref[...] = alpha * acc_ref[...] + pv
        m_ref[...] = m_new

    def v_tile(j, valid=None):
        jc = jnp.maximum(j, 0)
        vts = [vt_ref[0, u, jc] for u in range(1 if is_a else 2)]
        if valid is not None:
            vts = [jnp.where(valid, vt, jnp.zeros_like(vt)) for vt in vts]
        return vts

    def scores_to(slot, j):
        s = scores(jnp.maximum(j, 0))
        s_ref[slot, :, :2 * T] = s
        mt_ref[slot] = jnp.max(s, axis=0, keepdims=True)

    nblk = T // BLK
    tab_sub = tab_ref[0, 0] if is_a else None
    tab_diag = tab_ref[0, 1]

    def diag_tile(s):
        pbs, mts = [], []
        for lb in range(2 * nblk):
            qb = lb % nblk
            lanes = slice(lb * BLK, (lb + 1) * BLK)
            parts = [s[:(qb - 1) * BLK, lanes]] if qb >= 2 else []
            if qb >= 1:
                below = s[(qb - 1) * BLK:qb * BLK, lanes]
                parts.append(below if tab_sub is None else below + tab_sub)
            parts.append(s[qb * BLK:(qb + 1) * BLK, lanes] + tab_diag)
            sb = parts[0] if len(parts) == 1 else jnp.concatenate(parts, axis=0)
            mt = jnp.max(sb, axis=0, keepdims=True)
            pb = jnp.exp2(sb - mt).astype(BF16)
            if qb < nblk - 1:
                pb = jnp.concatenate([pb, jnp.zeros((T - (qb + 1) * BLK, BLK), BF16)], axis=0)
            pbs.append(pb)
            mts.append(mt)
        acc_ref[...] = pv_dot(v_tile(qi), jnp.concatenate(pbs, axis=1))
        m_ref[...] = jnp.concatenate(mts, axis=1)

    s_diag = scores(qi)
    if is_a:
        s_near = scores(jnp.maximum(qi - 1, 0)) + jnp.where(qi >= 1, 0.0, NEG_INF)
        last = s_near[T - BLK:]
        last = jnp.concatenate(
            [last[:, :BLK] + tab_sub, last[:, BLK:T], last[:, T:T + BLK] + tab_sub,
             last[:, T + BLK:]], axis=1)
        s_near = jnp.concatenate([s_near[:T - BLK], last], axis=0)
        diag_tile(s_diag)
        top = qi - 2
        scores_to(0, top)
        softmax_pv(v_tile(qi - 1), s_near, jnp.max(s_near, axis=0, keepdims=True))
        n_far = jnp.maximum(qi - 1, 0)
    else:
        top = qi - 1
        scores_to(0, top)
        diag_tile(s_diag)
        need = None
        for half, qh in enumerate((q_lo, q_hi)):
            qn = jnp.sqrt(jnp.max(jnp.sum(qh * qh, axis=0, keepdims=True), axis=1, keepdims=True))
            m_min = jnp.min(m_ref[:, half * T:(half + 1) * T], axis=1, keepdims=True)
            bound = (SKIP_NORM_MARGIN * qn) * kn_ref[0, half] + SKIP_ABS_MARGIN
            nd = (nc_ref[0, half] + bound) > (m_min - SKIP_THRESH)
            need = nd if need is None else (need | nd)
        tiles = need.shape[0]
        jidx = lax.broadcasted_iota(jnp.int32, need.shape, 0)
        cand = jnp.where(need & (jidx < qi), jidx, qi)
        n_far = qi - jnp.min(cand.reshape(tiles * 8, LANES))

    def far_steps(t0, count):
        for u in range(count):
            t = t0 + u
            scores_to((u + 1) % 2, top - t - 1)
            valid = None if u % 2 == 0 else t < n_far
            softmax_pv(v_tile(top - t, valid), s_ref[u % 2, :, :2 * T], mt_ref[u % 2])

    done = 0
    left = n_far + (n_far & 1)
    for unroll in FAR_UNROLLS:
        trips = left // unroll

        def body(i, carry, unroll=unroll, done=done):
            far_steps(done + unroll * i, unroll)
            return carry

        lax.fori_loop(0, trips, body, 0)
        done = done + unroll * trips
        left = left - unroll * trips

    vd = acc_ref.shape[0] - ONES_ROWS
    o = acc_ref[:vd] * (1.0 / acc_ref[vd:vd + 1])
    if is_a:
        lam = lam_ref[0:1, 0:1]
        o2 = o[:, :T] - lam * o[:, T:]
        ms = jnp.mean(o2 * o2, axis=0, keepdims=True)
        o2 = o2 * lax.rsqrt(ms + LN_EPS) * ng_ref[...] * (1.0 - LAM_INIT)
    else:
        o2 = jnp.concatenate([o[:, :T], o[:, T:]], axis=0)
    y_ref[0] = (o2 * g_ref[0].astype(F32)).astype(BF16)


def _attn_call(qt, k, e, vt5, g, tab, lam, ng, kn, nc, *, is_a):
    b, d, s = qt.shape
    nq = s // T
    off = 0 if is_a else PAIRS
    kd = LANES if is_a else 2 * LANES
    v_rows = vt5.shape[3]
    in_specs = [
        pl.BlockSpec((1, LANES, T), lambda i, p, q: (i, p + off, q)),
        pl.BlockSpec((1, s, LANES), lambda i, p, q: (i, 0, p + off)),
    ]
    args = [qt, k]
    if not is_a:
        in_specs.append(pl.BlockSpec((1, s, LANES), lambda i, p, q: (i, 0, 0)))
        args.append(e)
    in_specs += [
        pl.BlockSpec((1, 1 if is_a else 2, s // TM, v_rows, TM), lambda i, p, q: (i, p, 0, 0, 0)),
        pl.BlockSpec((1, LANES, T), lambda i, p, q: (i, p + off, q)),
    ]
    args += [vt5, g]
    if is_a:
        in_specs.append(pl.BlockSpec((1, 2, BLK, BLK), lambda i, p, q: (p, 0, 0, 0)))
        in_specs.append(pl.BlockSpec((8, LANES), lambda i, p, q: (0, 0)))
        in_specs.append(pl.BlockSpec((LANES, 1), lambda i, p, q: (0, 0)))
        args += [tab, lam, ng]
    else:
        in_specs.append(pl.BlockSpec((1, 2, BLK, BLK), lambda i, p, q: (A_HEADS, 0, 0, 0)))
        stat_spec = pl.BlockSpec((1, 2, s // TM, 8, LANES), lambda i, p, q: (i, p, 0, 0, 0))
        in_specs += [stat_spec, stat_spec]
        args += [tab, kn, nc]
    return pl.pallas_call(
        functools.partial(_attn_kernel, is_a=is_a),
        grid=(b, PAIRS, nq),
        in_specs=in_specs,
        out_specs=pl.BlockSpec((1, LANES, T), lambda i, p, q: (i, p, q)),
        out_shape=jax.ShapeDtypeStruct((b, PAIRS * LANES, s), BF16),
        scratch_shapes=[
            pltpu.VMEM((kd, 2 * T), BF16),
            pltpu.VMEM((1, 2 * T), F32),
            pltpu.VMEM((v_rows, 2 * T), F32),
            pltpu.VMEM((2, T, 2 * T + LANES), F32),
            pltpu.VMEM((2, 1, 2 * T), F32),
        ],
        compiler_params=pltpu.CompilerParams(
            dimension_semantics=("arbitrary", "arbitrary", "arbitrary"),
            vmem_limit_bytes=VMEM_LIMIT),
        name="attn_a" if is_a else "attn_b",
    )(*args)


def _out_kernel(ya_ref, yb_ref, w1_ref, w2_ref, x_ref, gate_ref, lng_ref, lnb_ref, o_ref):
    tn = (((0,), (0,)), ((), ()))
    y = (lax.dot_general(ya_ref[0], w1_ref[...], tn, preferred_element_type=F32)
         + lax.dot_general(yb_ref[0], w2_ref[...], tn, preferred_element_type=F32))
    z = DEEPNORM_ALPHA * x_ref[0] + gate_ref[0] * y
    mu = jnp.mean(z, axis=-1, keepdims=True)
    zc = z - mu
    var = jnp.mean(zc * zc, axis=-1, keepdims=True)
    o_ref[0] = zc * lax.rsqrt(var + LN_EPS) * lng_ref[...] + lnb_ref[...]


def _out_call(ya, yb, w1, w2, x, gate, lng, lnb):
    b, s, d = x.shape
    half = ya.shape[1]
    return pl.pallas_call(
        _out_kernel,
        grid=(b, s // TO),
        in_specs=[
            pl.BlockSpec((1, half, TO), lambda i, t: (i, 0, t)),
            pl.BlockSpec((1, half, TO), lambda i, t: (i, 0, t)),
            pl.BlockSpec((half, d), lambda i, t: (0, 0)),
            pl.BlockSpec((half, d), lambda i, t: (0, 0)),
            pl.BlockSpec((1, TO, d), lambda i, t: (i, t, 0)),
            pl.BlockSpec((1, 1, d), lambda i, t: (i, 0, 0)),
            pl.BlockSpec((1, d), lambda i, t: (0, 0)),
            pl.BlockSpec((1, d), lambda i, t: (0, 0)),
        ],
        out_specs=pl.BlockSpec((1, TO, d), lambda i, t: (i, t, 0)),
        out_shape=jax.ShapeDtypeStruct((b, s, d), x.dtype),
        compiler_params=pltpu.CompilerParams(
            dimension_semantics=("arbitrary", "arbitrary"), vmem_limit_bytes=VMEM_LIMIT),
        name="out",
    )(ya, yb, w1, w2, x, gate, lng, lnb)


def kernel(x, c, w_in, w_out, rel_bias, lam_q1, lam_k1, lam_q2, lam_k2, diff_norm_g,
           b_forget, w_ada, b_ada, ln_g, ln_b):
    b, s, d = x.shape
    assert s % T == 0 and s % TO == 0 and T == TM and d == PAIRS * 2 * LANES
    layer = 0
    aw = PAIRS * LANES

    c8 = jnp.pad(c, ((0, 8 - b), (0, 0)))
    ada8, lam = _ada_call(c8, w_ada[layer], b_ada[layer][None], lam_q1[layer][None],
                          lam_k1[layer][None], lam_q2[layer][None], lam_k2[layer][None])
    ada = ada8[:b]
    shift = ada[:, None, :d]
    scale = ada[:, None, d:2 * d]
    gate = ada[:, None, 2 * d:]

    w = w_in[layer]
    grp = lambda i: w[:, i * aw:(i + 1) * aw]
    order = jnp.argsort(b_forget[layer])
    by_head = lambda m: m.reshape(d, B_HEADS, HALF)[:, order].reshape(d, aw)
    wbf = jnp.pad(w[:, 8 * aw:][:, order], ((0, 0), (0, B_HEADS)))
    wrow = jnp.concatenate([grp(1), by_head(grp(5))], axis=1).astype(BF16)
    wt = jnp.concatenate([grp(0), by_head(grp(4)), grp(2), by_head(grp(6)), grp(3),
                          by_head(grp(7)), wbf], axis=1).T.astype(BF16)
    bfb = b_forget[layer][order].reshape(B_HEADS, 1).astype(F32)

    k, g, qt, vta, vtb, e, kn, nc = _proj_call(x, shift, scale, wrow, wt, bfb)
    tab = _table_call(rel_bias)
    ng = diff_norm_g[layer].reshape(LANES, 1)
    ya = _attn_call(qt, k, None, vta, g, tab, lam, ng, None, None, is_a=True)
    yb = _attn_call(qt, k, e, vtb, g, tab, None, None, kn, nc, is_a=False)

    wo = w_out[layer].astype(BF16)
    wo_b = wo[aw:].reshape(B_HEADS, HALF, d)[order].reshape(aw, d)
    return _out_call(ya, yb, wo[:aw], wo_b, x, gate, ln_g[layer][None], ln_b[layer][None])
```

```python
import functools
import math

import jax
import jax.numpy as jnp
from jax import lax
from jax.experimental import pallas as pl
from jax.experimental.pallas import tpu as pltpu

F32 = jnp.float32
BF16 = jnp.bfloat16

LANES = 128
HALF = 64
A_HEADS = 4
B_HEADS = 8
PAIRS = 4
CHUNK = 64
N_BUCKETS = 32
MAX_DISTANCE = 128
LN_EPS = 1e-5
NEG_INF = -1e30
LOG2E = 1.4426950408889634
DEPTH = 1
DEEPNORM_ALPHA = (2 * DEPTH) ** 0.25
LAM_INIT = 0.8 - 0.6 * math.exp(-0.3 * 0)

T = 512
BLK = 128
TM = 512
TO = 2048
SKIP_THRESH = 150.5
SKIP_NORM_MARGIN = 1.01
SKIP_ABS_MARGIN = 0.25
FAR_UNROLLS = (8, 4, 2)
ONES_ROWS = 16
VA_ROWS = LANES + ONES_ROWS
VB_ROWS = HALF + ONES_ROWS
VMEM_LIMIT = 60 * 1024 * 1024


def _ada_kernel(c_ref, w_ref, b_ref, q1_ref, k1_ref, q2_ref, k2_ref, ada_ref, lam_ref):
    c = c_ref[...]
    sc = c * jax.nn.sigmoid(c)
    ada_ref[...] = jnp.dot(sc, w_ref[...], precision=lax.Precision.HIGHEST,
                           preferred_element_type=F32) + b_ref[...]
    s1 = jnp.sum(q1_ref[...] * k1_ref[...], axis=-1, keepdims=True)
    s2 = jnp.sum(q2_ref[...] * k2_ref[...], axis=-1, keepdims=True)
    lam = jnp.exp(s1) - jnp.exp(s2) + LAM_INIT
    lam_ref[...] = jnp.broadcast_to(lam, lam_ref.shape)


def _ada_call(c8, w_ada, b_ada, q1, k1, q2, k2):
    d3 = w_ada.shape[1]
    return pl.pallas_call(
        _ada_kernel,
        out_shape=(jax.ShapeDtypeStruct((c8.shape[0], d3), F32),
                   jax.ShapeDtypeStruct((8, LANES), F32)),
        compiler_params=pltpu.CompilerParams(vmem_limit_bytes=VMEM_LIMIT),
        name="ada",
    )(c8, w_ada, b_ada, q1, k1, q2, k2)


def _log_bucket_starts():
    nb = N_BUCKETS // 2
    e = nb // 2
    w = nb - e
    starts, n = [], e
    for j in range(1, w):
        while n ** w * e ** j < e ** w * MAX_DISTANCE ** j:
            n += 1
        starts.append(n)
    return starts


def _t5_bucket(rel):
    nb = N_BUCKETS // 2
    ret = jnp.where(rel > 0, nb, 0)
    n = jnp.abs(rel)
    max_exact = nb // 2
    large = jnp.full_like(n, max_exact)
    for start in _log_bucket_starts():
        large = large + jnp.where(n >= start, 1, 0)
    return ret + jnp.where(n < max_exact, n, large)


def _table_kernel(rb_ref, tab_ref):
    i = pl.program_id(0)
    h = jnp.minimum(i, A_HEADS - 1)
    kl = lax.broadcasted_iota(jnp.int32, (BLK, BLK), 0)
    ql = lax.broadcasted_iota(jnp.int32, (BLK, BLK), 1)
    far = rb_ref[N_BUCKETS // 2 - 1, h]
    for d in range(2):
        bucket = _t5_bucket(kl - ql + (d - 1) * BLK)
        bias = jnp.zeros((BLK, BLK), F32)
        for bk in range(N_BUCKETS):
            bias = jnp.where(bucket == bk, rb_ref[bk, h], bias)
        tab_a = (bias - far) * LOG2E
        if d == 1:
            tab_a = jnp.where((kl // CHUNK) <= (ql // CHUNK), tab_a, NEG_INF)
            tab_b = jnp.where(kl <= ql, 0.0, NEG_INF)
        else:
            tab_b = jnp.zeros((BLK, BLK), F32)
        tab_ref[0, d] = jnp.where(i < A_HEADS, tab_a, tab_b)


def _table_call(rel_bias):
    return pl.pallas_call(
        _table_kernel,
        grid=(A_HEADS + 1,),
        in_specs=[pl.BlockSpec(memory_space=pltpu.SMEM)],
        out_specs=pl.BlockSpec((1, 2, BLK, BLK), lambda i: (i, 0, 0, 0)),
        out_shape=jax.ShapeDtypeStruct((A_HEADS + 1, 2, BLK, BLK), F32),
        compiler_params=pltpu.CompilerParams(
            dimension_semantics=("arbitrary",), vmem_limit_bytes=VMEM_LIMIT),
        name="tables",
    )(rel_bias)


def _split3(v):
    hi = v.astype(BF16)
    r1 = v - hi.astype(F32)
    mid = r1.astype(BF16)
    lo = (r1 - mid.astype(F32)).astype(BF16)
    return hi, mid, lo


def _proj_kernel(x_ref, sh_ref, sc_ref, wrow_ref, wt_ref, bfb_ref,
                 k_ref, g_ref, qt_ref, vta_ref, vtb_ref, e_ref, kn_ref, nc_ref, carry_ref):
    t = pl.program_id(1)
    d = x_ref.shape[2]
    x = x_ref[0]
    mu = jnp.mean(x, axis=-1, keepdims=True)
    xc = x - mu
    var = jnp.mean(xc * xc, axis=-1, keepdims=True)
    h = (xc * lax.rsqrt(var + LN_EPS)) * (1.0 + sc_ref[0]) + sh_ref[0]
    hb = h.astype(BF16)

    kb = jnp.dot(hb, wrow_ref[...], preferred_element_type=F32).astype(BF16)
    k_ref[0] = kb

    tt = lax.dot_general(wt_ref[...], hb, (((1,), (1,)), ((), ())),
                         preferred_element_type=F32)
    qt_ref[0] = (tt[:d] * (LOG2E * HALF ** -0.5)).astype(BF16)
    g = tt[2 * d:3 * d]
    g_ref[0] = (g * jax.nn.sigmoid(g)).astype(BF16)

    z = tt[3 * d:3 * d + B_HEADS] + bfb_ref[...]
    logf = jnp.minimum(z, 0.0) - jnp.log1p(jnp.exp(-jnp.abs(z)))
    ri = lax.broadcasted_iota(jnp.int32, (TM, TM), 0)
    ci = lax.broadcasted_iota(jnp.int32, (TM, TM), 1)
    upper = jnp.where(ri <= ci, 1.0, 0.0).astype(BF16)
    hi, mid, lo = _split3(logf)
    pieces = jnp.concatenate([hi.astype(F32), mid.astype(F32), lo.astype(F32),
                              jnp.zeros((B_HEADS, TM), F32)], axis=0)
    cs = jnp.dot(pieces.astype(BF16), upper, preferred_element_type=F32)

    @pl.when(t == 0)
    def _():
        carry_ref[...] = jnp.zeros_like(carry_ref)

    carry = jnp.concatenate([carry_ref[...]] * (TM // LANES), axis=1)
    cf = (cs[:B_HEADS] + cs[B_HEADS:2 * B_HEADS]) + cs[2 * B_HEADS:3 * B_HEADS] + carry
    carry_ref[...] = jnp.broadcast_to(cf[:, TM - 1:TM], carry_ref.shape)
    ncf = -cf * LOG2E
    vh, vm, vl = _split3(ncf)
    pt = jnp.concatenate([vh.astype(F32), vm.astype(F32), vl.astype(F32),
                          jnp.zeros((LANES - 3 * B_HEADS, TM), F32)], axis=0)
    e_ref[0] = pt.T.astype(BF16)

    kf = kb[:, d // 2:].astype(F32)
    gc = lax.broadcasted_iota(jnp.int32, (d // 2, LANES), 0)
    gh = lax.broadcasted_iota(jnp.int32, (d // 2, LANES), 1)
    gsel = jnp.where(gc // HALF == gh, 1.0, 0.0).astype(BF16)
    ksq = jnp.dot((kf * kf).astype(BF16), gsel, preferred_element_type=F32)
    knt = jnp.sqrt(jnp.max(ksq, axis=0, keepdims=True))
    nct = jnp.max(ncf, axis=1, keepdims=True)
    for hh in range(B_HEADS):
        kn_ref[0, hh, 0] = jnp.broadcast_to(knt[:, hh:hh + 1], (8, LANES))
        nc_ref[0, hh, 0] = jnp.broadcast_to(nct[hh:hh + 1, :], (8, LANES))

    hd = d // 2
    vta_ref[0, :, 0, :LANES, :] = tt[d:d + hd].reshape(A_HEADS, LANES, TM).astype(BF16)
    vtb_ref[0, :, 0, :HALF, :] = tt[d + hd:2 * d].reshape(B_HEADS, HALF, TM).astype(BF16)
    vta_ref[0, :, 0, LANES:, :] = jnp.ones((A_HEADS, ONES_ROWS, TM), BF16)
    vtb_ref[0, :, 0, HALF:, :] = jnp.ones((B_HEADS, ONES_ROWS, TM), BF16)


def _proj_call(x, shift, scale, wrow, wt, bfb):
    b, s, d = x.shape
    nt = s // TM
    nrow = wrow.shape[1]
    return pl.pallas_call(
        _proj_kernel,
        grid=(b, nt),
        in_specs=[
            pl.BlockSpec((1, TM, d), lambda i, t: (i, t, 0)),
            pl.BlockSpec((1, 1, d), lambda i, t: (i, 0, 0)),
            pl.BlockSpec((1, 1, d), lambda i, t: (i, 0, 0)),
            pl.BlockSpec((d, nrow), lambda i, t: (0, 0)),
            pl.BlockSpec((wt.shape[0], d), lambda i, t: (0, 0)),
            pl.BlockSpec((B_HEADS, 1), lambda i, t: (0, 0)),
        ],
        out_specs=(
            pl.BlockSpec((1, TM, d), lambda i, t: (i, t, 0)),
            pl.BlockSpec((1, d, TM), lambda i, t: (i, 0, t)),
            pl.BlockSpec((1, d, TM), lambda i, t: (i, 0, t)),
            pl.BlockSpec((1, A_HEADS, 1, VA_ROWS, TM), lambda i, t: (i, 0, t, 0, 0)),
            pl.BlockSpec((1, B_HEADS, 1, VB_ROWS, TM), lambda i, t: (i, 0, t, 0, 0)),
            pl.BlockSpec((1, TM, LANES), lambda i, t: (i, t, 0)),
            pl.BlockSpec((1, B_HEADS, 1, 8, LANES), lambda i, t: (i, 0, t, 0, 0)),
            pl.BlockSpec((1, B_HEADS, 1, 8, LANES), lambda i, t: (i, 0, t, 0, 0)),
        ),
        out_shape=(
            jax.ShapeDtypeStruct((b, s, d), BF16),
            jax.ShapeDtypeStruct((b, d, s), BF16),
            jax.ShapeDtypeStruct((b, d, s), BF16),
            jax.ShapeDtypeStruct((b, A_HEADS, nt, VA_ROWS, TM), BF16),
            jax.ShapeDtypeStruct((b, B_HEADS, nt, VB_ROWS, TM), BF16),
            jax.ShapeDtypeStruct((b, s, LANES), BF16),
            jax.ShapeDtypeStruct((b, B_HEADS, nt, 8, LANES), F32),
            jax.ShapeDtypeStruct((b, B_HEADS, nt, 8, LANES), F32),
        ),
        scratch_shapes=[pltpu.VMEM((B_HEADS, LANES), F32)],
        compiler_params=pltpu.CompilerParams(
            dimension_semantics=("arbitrary", "arbitrary"), vmem_limit_bytes=VMEM_LIMIT),
        name="proj",
    )(x, shift, scale, wrow, wt, bfb)


def _attn_kernel(*refs, is_a):
    if is_a:
        (qt_ref, k_ref, vt_ref, g_ref, tab_ref, lam_ref, ng_ref,
         y_ref, qtb_ref, m_ref, acc_ref, s_ref, mt_ref) = refs
        e_ref = None
    else:
        (qt_ref, k_ref, e_ref, vt_ref, g_ref, tab_ref, kn_ref, nc_ref,
         y_ref, qtb_ref, m_ref, acc_ref, s_ref, mt_ref) = refs
    p = pl.program_id(1)
    qi = pl.program_id(2)

    qt = qt_ref[0].astype(F32)
    row = lax.broadcasted_iota(jnp.int32, (LANES, T), 0)
    q_lo = jnp.where(row < HALF, qt, 0.0)
    q_hi = jnp.where(row >= HALF, qt, 0.0)
    qtb_ref[:LANES, :] = jnp.concatenate([q_lo, q_hi], axis=1).astype(BF16)
    if not is_a:
        r2 = lax.broadcasted_iota(jnp.int32, (LANES, 2 * T), 0)
        c2 = lax.broadcasted_iota(jnp.int32, (LANES, 2 * T), 1)
        head = 2 * p + jnp.where(c2 >= T, 1, 0)
        sel = (r2 == head) | (r2 == head + B_HEADS) | (r2 == head + 2 * B_HEADS)
        qtb_ref[LANES:, :] = jnp.where(sel, 1.0, 0.0).astype(BF16)

    def scores(j):
        rows = pl.ds(pl.multiple_of(j * T, T), T)
        kt = k_ref[0, rows, :]
        if not is_a:
            kt = jnp.concatenate([kt, e_ref[0, rows, :]], axis=1)
        return jnp.dot(kt, qtb_ref[...], preferred_element_type=F32)

    def pv_dot(vts, pb):
        if is_a:
            return jnp.dot(vts[0], pb, preferred_element_type=F32)
        return jnp.concatenate(
            [jnp.dot(vts[0], pb[:, :T], preferred_element_type=F32),
             jnp.dot(vts[1], pb[:, T:], preferred_element_type=F32)], axis=1)

    def softmax_pv(vts, s, mt):
        m_old = m_ref[...]
        m_new = jnp.maximum(m_old, mt)
        alpha = jnp.exp2(m_old - m_new)
        pv = pv_dot(vts, jnp.exp2(s - m_new).astype(BF16))
        acc_ref[...] = alpha * acc_ref[...] + pv
        m_ref[...] = m_new

    def v_tile(j, valid=None):
        jc = jnp.maximum(j, 0)
        vts = [vt_ref[0, u, jc] for u in range(1 if is_a else 2)]
        if valid is not None:
            vts = [jnp.where(valid, vt, jnp.zeros_like(vt)) for vt in vts]
        return vts

    def scores_to(slot, j):
        s = scores(jnp.maximum(j, 0))
        s_ref[slot, :, :2 * T] = s
        mt_ref[slot] = jnp.max(s, axis=0, keepdims=True)

    nblk = T // BLK
    tab_sub = tab_ref[0, 0] if is_a else None
    tab_diag = tab_ref[0, 1]

    def diag_tile(s):
        pbs, mts = [], []
        for lb in range(2 * nblk):
            qb = lb % nblk
            lanes = slice(lb * BLK, (lb + 1) * BLK)
            parts = [s[:(qb - 1) * BLK, lanes]] if qb >= 2 else []
            if qb >= 1:
                below = s[(qb - 1) * BLK:qb * BLK, lanes]
                parts.append(below if tab_sub is None else below + tab_sub)
            parts.append(s[qb * BLK:(qb + 1) * BLK, lanes] + tab_diag)
            sb = parts[0] if len(parts) == 1 else jnp.concatenate(parts, axis=0)
            mt = jnp.max(sb, axis=0, keepdims=True)
            pb = jnp.exp2(sb - mt).astype(BF16)
            if qb < nblk - 1:
                pb = jnp.concatenate([pb, jnp.zeros((T - (qb + 1) * BLK, BLK), BF16)], axis=0)
            pbs.append(pb)
            mts.append(mt)
        acc_ref[...] = pv_dot(v_tile(qi), jnp.concatenate(pbs, axis=1))
        m_ref[...] = jnp.concatenate(mts, axis=1)

    s_diag = scores(qi)
    if is_a:
        s_near = scores(jnp.maximum(qi - 1, 0)) + jnp.where(qi >= 1, 0.0, NEG_INF)
        last = s_near[T - BLK:]
        last = jnp.concatenate(
            [last[:, :BLK] + tab_sub, last[:, BLK:T], last[:, T:T + BLK] + tab_sub,
             last[:, T + BLK:]], axis=1)
        s_near = jnp.concatenate([s_near[:T - BLK], last], axis=0)
        diag_tile(s_diag)
        top = qi - 2
        scores_to(0, top)
        softmax_pv(v_tile(qi - 1), s_near, jnp.max(s_near, axis=0, keepdims=True))
        n_far = jnp.maximum(qi - 1, 0)
    else:
        top = qi - 1
        scores_to(0, top)
        diag_tile(s_diag)
        need = None
        for half, qh in enumerate((q_lo, q_hi)):
            qn = jnp.sqrt(jnp.max(jnp.sum(qh * qh, axis=0, keepdims=True), axis=1, keepdims=True))
            m_min = jnp.min(m_ref[:, half * T:(half + 1) * T], axis=1, keepdims=True)
            bound = (SKIP_NORM_MARGIN * qn) * kn_ref[0, half] + SKIP_ABS_MARGIN
            nd = (nc_ref[0, half] + bound) > (m_min - SKIP_THRESH)
            need = nd if need is None else (need | nd)
        tiles = need.shape[0]
        jidx = lax.broadcasted_iota(jnp.int32, need.shape, 0)
        cand = jnp.where(need & (jidx < qi), jidx, qi)
        n_far = qi - jnp.min(cand.reshape(tiles * 8, LANES))

    def far_steps(t0, count):
        for u in range(count):
            t = t0 + u
            scores_to((u + 1) % 2, top - t - 1)
            valid = None if u % 2 == 0 else t < n_far
            softmax_pv(v_tile(top - t, valid), s_ref[u % 2, :, :2 * T], mt_ref[u % 2])

    done = 0
    left = n_far + (n_far & 1)
    for unroll in FAR_UNROLLS:
        trips = left // unroll

        def body(i, carry, unroll=unroll, done=done):
            far_steps(done + unroll * i, unroll)
            return carry

        lax.fori_loop(0, trips, body, 0)
        done = done + unroll * trips
        left = left - unroll * trips

    vd = acc_ref.shape[0] - ONES_ROWS
    o = acc_ref[:vd] * (1.0 / acc_ref[vd:vd + 1])
    if is_a:
        lam = lam_ref[0:1, 0:1]
        o2 = o[:, :T] - lam * o[:, T:]
        ms = jnp.mean(o2 * o2, axis=0, keepdims=True)
        o2 = o2 * lax.rsqrt(ms + LN_EPS) * ng_ref[...] * (1.0 - LAM_INIT)
    else:
        o2 = jnp.concatenate([o[:, :T], o[:, T:]], axis=0)
    y_ref[0] = (o2 * g_ref[0].astype(F32)).astype(BF16)


def _attn_call(qt, k, e, vt5, g, tab, lam, ng, kn, nc, *, is_a):
    b, d, s = qt.shape
    nq = s // T
    off = 0 if is_a else PAIRS
    kd = LANES if is_a else 2 * LANES
    v_rows = vt5.shape[3]
    in_specs = [
        pl.BlockSpec((1, LANES, T), lambda i, p, q: (i, p + off, q)),
        pl.BlockSpec((1, s, LANES), lambda i, p, q: (i, 0, p + off)),
    ]
    args = [qt, k]
    if not is_a:
        in_specs.append(pl.BlockSpec((1, s, LANES), lambda i, p, q: (i, 0, 0)))
        args.append(e)
    in_specs += [
        pl.BlockSpec((1, 1 if is_a else 2, s // TM, v_rows, TM), lambda i, p, q: (i, p, 0, 0, 0)),
        pl.BlockSpec((1, LANES, T), lambda i, p, q: (i, p + off, q)),
    ]
    args += [vt5, g]
    if is_a:
        in_specs.append(pl.BlockSpec((1, 2, BLK, BLK), lambda i, p, q: (p, 0, 0, 0)))
        in_specs.append(pl.BlockSpec((8, LANES), lambda i, p, q: (0, 0)))
        in_specs.append(pl.BlockSpec((LANES, 1), lambda i, p, q: (0, 0)))
        args += [tab, lam, ng]
    else:
        in_specs.append(pl.BlockSpec((1, 2, BLK, BLK), lambda i, p, q: (A_HEADS, 0, 0, 0)))
        stat_spec = pl.BlockSpec((1, 2, s // TM, 8, LANES), lambda i, p, q: (i, p, 0, 0, 0))
        in_specs += [stat_spec, stat_spec]
        args += [tab, kn, nc]
    return pl.pallas_call(
        functools.partial(_attn_kernel, is_a=is_a),
        grid=(b, PAIRS, nq),
        in_specs=in_specs,
        out_specs=pl.BlockSpec((1, LANES, T), lambda i, p, q: (i, p, q)),
        out_shape=jax.ShapeDtypeStruct((b, PAIRS * LANES, s), BF16),
        scratch_shapes=[
            pltpu.VMEM((kd, 2 * T), BF16),
            pltpu.VMEM((1, 2 * T), F32),
            pltpu.VMEM((v_rows, 2 * T), F32),
            pltpu.VMEM((2, T, 2 * T + LANES), F32),
            pltpu.VMEM((2, 1, 2 * T), F32),
        ],
        compiler_params=pltpu.CompilerParams(
            dimension_semantics=("arbitrary", "arbitrary", "arbitrary"),
            vmem_limit_bytes=VMEM_LIMIT),
        name="attn_a" if is_a else "attn_b",
    )(*args)


def _out_kernel(ya_ref, yb_ref, w1_ref, w2_ref, x_ref, gate_ref, lng_ref, lnb_ref, o_ref):
    tn = (((0,), (0,)), ((), ()))
    y = (lax.dot_general(ya_ref[0], w1_ref[...], tn, preferred_element_type=F32)
         + lax.dot_general(yb_ref[0], w2_ref[...], tn, preferred_element_type=F32))
    z = DEEPNORM_ALPHA * x_ref[0] + gate_ref[0] * y
    mu = jnp.mean(z, axis=-1, keepdims=True)
    zc = z - mu
    var = jnp.mean(zc * zc, axis=-1, keepdims=True)
    o_ref[0] = zc * lax.rsqrt(var + LN_EPS) * lng_ref[...] + lnb_ref[...]


def _out_call(ya, yb, w1, w2, x, gate, lng, lnb):
    b, s, d = x.shape
    half = ya.shape[1]
    return pl.pallas_call(
        _out_kernel,
        grid=(b, s // TO),
        in_specs=[
            pl.BlockSpec((1, half, TO), lambda i, t: (i, 0, t)),
            pl.BlockSpec((1, half, TO), lambda i, t: (i, 0, t)),
            pl.BlockSpec((half, d), lambda i, t: (0, 0)),
            pl.BlockSpec((half, d), lambda i, t: (0, 0)),
            pl.BlockSpec((1, TO, d), lambda i, t: (i, t, 0)),
            pl.BlockSpec((1, 1, d), lambda i, t: (i, 0, 0)),
            pl.BlockSpec((1, d), lambda i, t: (0, 0)),
            pl.BlockSpec((1, d), lambda i, t: (0, 0)),
        ],
        out_specs=pl.BlockSpec((1, TO, d), lambda i, t: (i, t, 0)),
        out_shape=jax.ShapeDtypeStruct((b, s, d), x.dtype),
        compiler_params=pltpu.CompilerParams(
            dimension_semantics=("arbitrary", "arbitrary"), vmem_limit_bytes=VMEM_LIMIT),
        name="out",
    )(ya, yb, w1, w2, x, gate, lng, lnb)


def kernel(x, c, w_in, w_out, rel_bias, lam_q1, lam_k1, lam_q2, lam_k2, diff_norm_g,
           b_forget, w_ada, b_ada, ln_g, ln_b):
    b, s, d = x.shape
    assert s % T == 0 and s % TO == 0 and T == TM and d == PAIRS * 2 * LANES
    layer = 0
    aw = PAIRS * LANES

    c8 = jnp.pad(c, ((0, 8 - b), (0, 0)))
    ada8, lam = _ada_call(c8, w_ada[layer], b_ada[layer][None], lam_q1[layer][None],
                          lam_k1[layer][None], lam_q2[layer][None], lam_k2[layer][None])
    ada = ada8[:b]
    shift = ada[:, None, :d]
    scale = ada[:, None, d:2 * d]
    gate = ada[:, None, 2 * d:]

    w = w_in[layer]
    grp = lambda i: w[:, i * aw:(i + 1) * aw]
    order = jnp.argsort(b_forget[layer])
    by_head = lambda m: m.reshape(d, B_HEADS, HALF)[:, order].reshape(d, aw)
    wbf = jnp.pad(w[:, 8 * aw:][:, order], ((0, 0), (0, B_HEADS)))
    wrow = jnp.concatenate([grp(1), by_head(grp(5))], axis=1).astype(BF16)
    wt = jnp.concatenate([grp(0), by_head(grp(4)), grp(2), by_head(grp(6)), grp(3),
                          by_head(grp(7)), wbf], axis=1).T.astype(BF16)
    bfb = b_forget[layer][order].reshape(B_HEADS, 1).astype(F32)

    k, g, qt, vta, vtb, e, kn, nc = _proj_call(x, shift, scale, wrow, wt, bfb)
    tab = _table_call(rel_bias)
    ng = diff_norm_g[layer].reshape(LANES, 1)
    ya = _attn_call(qt, k, None, vta, g, tab, lam, ng, None, None, is_a=True)
    yb = _attn_call(qt, k, e, vtb, g, tab, None, None, kn, nc, is_a=False)

    wo = w_out[layer].astype(BF16)
    wo_b = wo[aw:].reshape(B_HEADS, HALF, d)[order].reshape(aw, d)
    return _out_call(ya, yb, wo[:aw], wo_b, x, gate, ln_g[layer][None], ln_b[layer][None])
```

```python
import functools
import math

import jax
import jax.numpy as jnp
from jax import lax
from jax.experimental import pallas as pl
from jax.experimental.pallas import tpu as pltpu

F32 = jnp.float32
BF16 = jnp.bfloat16

LANES = 128
HALF = 64
A_HEADS = 4
B_HEADS = 8
PAIRS = 4
CHUNK = 64
N_BUCKETS = 32
MAX_DISTANCE = 128
LN_EPS = 1e-5
NEG_INF = -1e30
LOG2E = 1.4426950408889634
DEPTH = 1
DEEPNORM_ALPHA = (2 * DEPTH) ** 0.25
LAM_INIT = 0.8 - 0.6 * math.exp(-0.3 * 0)

T = 512
BLK = 128
TM = 512
TO = 2048
SKIP_THRESH = 150.5
SKIP_NORM_MARGIN = 1.01
SKIP_ABS_MARGIN = 0.25
FAR_UNROLLS = (8, 4, 2)
ONES_ROWS = 16
VA_ROWS = LANES + ONES_ROWS
VB_ROWS = HALF + ONES_ROWS
VMEM_LIMIT = 60 * 1024 * 1024


def _ada_kernel(c_ref, w_ref, b_ref, q1_ref, k1_ref, q2_ref, k2_ref, ada_ref, lam_ref):
    c = c_ref[...]
    sc = c * jax.nn.sigmoid(c)
    ada_ref[...] = jnp.dot(sc, w_ref[...], precision=lax.Precision.HIGHEST,
                           preferred_element_type=F32) + b_ref[...]
    s1 = jnp.sum(q1_ref[...] * k1_ref[...], axis=-1, keepdims=True)
    s2 = jnp.sum(q2_ref[...] * k2_ref[...], axis=-1, keepdims=True)
    lam = jnp.exp(s1) - jnp.exp(s2) + LAM_INIT
    lam_ref[...] = jnp.broadcast_to(lam, lam_ref.shape)


def _ada_call(c8, w_ada, b_ada, q1, k1, q2, k2):
    d3 = w_ada.shape[1]
    return pl.pallas_call(
        _ada_kernel,
        out_shape=(jax.ShapeDtypeStruct((c8.shape[0], d3), F32),
                   jax.ShapeDtypeStruct((8, LANES), F32)),
        compiler_params=pltpu.CompilerParams(vmem_limit_bytes=VMEM_LIMIT),
        name="ada",
    )(c8, w_ada, b_ada, q1, k1, q2, k2)


def _log_bucket_starts():
    nb = N_BUCKETS // 2
    e = nb // 2
    w = nb - e
    starts, n = [], e
    for j in range(1, w):
        while n ** w * e ** j < e ** w * MAX_DISTANCE ** j:
            n += 1
        starts.append(n)
    return starts


def _t5_bucket(rel):
    nb = N_BUCKETS // 2
    ret = jnp.where(rel > 0, nb, 0)
    n = jnp.abs(rel)
    max_exact = nb // 2
    large = jnp.full_like(n, max_exact)
    for start in _log_bucket_starts():
        large = large + jnp.where(n >= start, 1, 0)
    return ret + jnp.where(n < max_exact, n, large)


def _table_kernel(rb_ref, tab_ref):
    i = pl.program_id(0)
    h = jnp.minimum(i, A_HEADS - 1)
    kl = lax.broadcasted_iota(jnp.int32, (BLK, BLK), 0)
    ql = lax.broadcasted_iota(jnp.int32, (BLK, BLK), 1)
    far = rb_ref[N_BUCKETS // 2 - 1, h]
    for d in range(2):
        bucket = _t5_bucket(kl - ql + (d - 1) * BLK)
        bias = jnp.zeros((BLK, BLK), F32)
        for bk in range(N_BUCKETS):
            bias = jnp.where(bucket == bk, rb_ref[bk, h], bias)
        tab_a = (bias - far) * LOG2E
        if d == 1:
            tab_a = jnp.where((kl // CHUNK) <= (ql // CHUNK), tab_a, NEG_INF)
            tab_b = jnp.where(kl <= ql, 0.0, NEG_INF)
        else:
            tab_b = jnp.zeros((BLK, BLK), F32)
        tab_ref[0, d] = jnp.where(i < A_HEADS, tab_a, tab_b)


def _table_call(rel_bias):
    return pl.pallas_call(
        _table_kernel,
        grid=(A_HEADS + 1,),
        in_specs=[pl.BlockSpec(memory_space=pltpu.SMEM)],
        out_specs=pl.BlockSpec((1, 2, BLK, BLK), lambda i: (i, 0, 0, 0)),
        out_shape=jax.ShapeDtypeStruct((A_HEADS + 1, 2, BLK, BLK), F32),
        compiler_params=pltpu.CompilerParams(
            dimension_semantics=("arbitrary",), vmem_limit_bytes=VMEM_LIMIT),
        name="tables",
    )(rel_bias)


def _split3(v):
    hi = v.astype(BF16)
    r1 = v - hi.astype(F32)
    mid = r1.astype(BF16)
    lo = (r1 - mid.astype(F32)).astype(BF16)
    return hi, mid, lo


def _proj_kernel(x_ref, sh_ref, sc_ref, wrow_ref, wt_ref, bfb_ref,
                 k_ref, g_ref, qt_ref, vta_ref, vtb_ref, e_ref, kn_ref, nc_ref, carry_ref):
    t = pl.program_id(1)
    d = x_ref.shape[2]
    x = x_ref[0]
    mu = jnp.mean(x, axis=-1, keepdims=True)
    xc = x - mu
    var = jnp.mean(xc * xc, axis=-1, keepdims=True)
    h = (xc * lax.rsqrt(var + LN_EPS)) * (1.0 + sc_ref[0]) + sh_ref[0]
    hb = h.astype(BF16)

    kb = jnp.dot(hb, wrow_ref[...], preferred_element_type=F32).astype(BF16)
    k_ref[0] = kb

    tt = lax.dot_general(wt_ref[...], hb, (((1,), (1,)), ((), ())),
                         preferred_element_type=F32)
    qt_ref[0] = (tt[:d] * (LOG2E * HALF ** -0.5)).astype(BF16)
    g = tt[2 * d:3 * d]
    g_ref[0] = (g * jax.nn.sigmoid(g)).astype(BF16)

    z = tt[3 * d:3 * d + B_HEADS] + bfb_ref[...]
    logf = jnp.minimum(z, 0.0) - jnp.log1p(jnp.exp(-jnp.abs(z)))
    ri = lax.broadcasted_iota(jnp.int32, (TM, TM), 0)
    ci = lax.broadcasted_iota(jnp.int32, (TM, TM), 1)
    upper = jnp.where(ri <= ci, 1.0, 0.0).astype(BF16)
    hi, mid, lo = _split3(logf)
    pieces = jnp.concatenate([hi.astype(F32), mid.astype(F32), lo.astype(F32),
                              jnp.zeros((B_HEADS, TM), F32)], axis=0)
    cs = jnp.dot(pieces.astype(BF16), upper, preferred_element_type=F32)

    @pl.when(t == 0)
    def _():
        carry_ref[...] = jnp.zeros_like(carry_ref)

    carry = jnp.concatenate([carry_ref[...]] * (TM // LANES), axis=1)
    cf = (cs[:B_HEADS] + cs[B_HEADS:2 * B_HEADS]) + cs[2 * B_HEADS:3 * B_HEADS] + carry
    carry_ref[...] = jnp.broadcast_to(cf[:, TM - 1:TM], carry_ref.shape)
    ncf = -cf * LOG2E
    vh, vm, vl = _split3(ncf)
    pt = jnp.concatenate([vh.astype(F32), vm.astype(F32), vl.astype(F32),
                          jnp.zeros((LANES - 3 * B_HEADS, TM), F32)], axis=0)
    e_ref[0] = pt.T.astype(BF16)

    kf = kb[:, d // 2:].astype(F32)
    gc = lax.broadcasted_iota(jnp.int32, (d // 2, LANES), 0)
    gh = lax.broadcasted_iota(jnp.int32, (d // 2, LANES), 1)
    gsel = jnp.where(gc // HALF == gh, 1.0, 0.0).astype(BF16)
    ksq = jnp.dot((kf * kf).astype(BF16), gsel, preferred_element_type=F32)
    knt = jnp.sqrt(jnp.max(ksq, axis=0, keepdims=True))
    nct = jnp.max(ncf, axis=1, keepdims=True)
    for hh in range(B_HEADS):
        kn_ref[0, hh, 0] = jnp.broadcast_to(knt[:, hh:hh + 1], (8, LANES))
        nc_ref[0, hh, 0] = jnp.broadcast_to(nct[hh:hh + 1, :], (8, LANES))

    hd = d // 2
    vta_ref[0, :, 0, :LANES, :] = tt[d:d + hd].reshape(A_HEADS, LANES, TM).astype(BF16)
    vtb_ref[0, :, 0, :HALF, :] = tt[d + hd:2 * d].reshape(B_HEADS, HALF, TM).astype(BF16)
    vta_ref[0, :, 0, LANES:, :] = jnp.ones((A_HEADS, ONES_ROWS, TM), BF16)
    vtb_ref[0, :, 0, HALF:, :] = jnp.ones((B_HEADS, ONES_ROWS, TM), BF16)


def _proj_call(x, shift, scale, wrow, wt, bfb):
    b, s, d = x.shape
    nt = s // TM
    nrow = wrow.shape[1]
    return pl.pallas_call(
        _proj_kernel,
        grid=(b, nt),
        in_specs=[
            pl.BlockSpec((1, TM, d), lambda i, t: (i, t, 0)),
            pl.BlockSpec((1, 1, d), lambda i, t: (i, 0, 0)),
            pl.BlockSpec((1, 1, d), lambda i, t: (i, 0, 0)),
            pl.BlockSpec((d, nrow), lambda i, t: (0, 0)),
            pl.BlockSpec((wt.shape[0], d), lambda i, t: (0, 0)),
            pl.BlockSpec((B_HEADS, 1), lambda i, t: (0, 0)),
        ],
        out_specs=(
            pl.BlockSpec((1, TM, d), lambda i, t: (i, t, 0)),
            pl.BlockSpec((1, d, TM), lambda i, t: (i, 0, t)),
            pl.BlockSpec((1, d, TM), lambda i, t: (i, 0, t)),
            pl.BlockSpec((1, A_HEADS, 1, VA_ROWS, TM), lambda i, t: (i, 0, t, 0, 0)),
            pl.BlockSpec((1, B_HEADS, 1, VB_ROWS, TM), lambda i, t: (i, 0, t, 0, 0)),
            pl.BlockSpec((1, TM, LANES), lambda i, t: (i, t, 0)),
            pl.BlockSpec((1, B_HEADS, 1, 8, LANES), lambda i, t: (i, 0, t, 0, 0)),
            pl.BlockSpec((1, B_HEADS, 1, 8, LANES), lambda i, t: (i, 0, t, 0, 0)),
        ),
        out_shape=(
            jax.ShapeDtypeStruct((b, s, d), BF16),
            jax.ShapeDtypeStruct((b, d, s), BF16),
            jax.ShapeDtypeStruct((b, d, s), BF16),
            jax.ShapeDtypeStruct((b, A_HEADS, nt, VA_ROWS, TM), BF16),
            jax.ShapeDtypeStruct((b, B_HEADS, nt, VB_ROWS, TM), BF16),
            jax.ShapeDtypeStruct((b, s, LANES), BF16),
            jax.ShapeDtypeStruct((b, B_HEADS, nt, 8, LANES), F32),
            jax.ShapeDtypeStruct((b, B_HEADS, nt, 8, LANES), F32),
        ),
        scratch_shapes=[pltpu.VMEM((B_HEADS, LANES), F32)],
        compiler_params=pltpu.CompilerParams(
            dimension_semantics=("arbitrary", "arbitrary"), vmem_limit_bytes=VMEM_LIMIT),
        name="proj",
    )(x, shift, scale, wrow, wt, bfb)


def _attn_kernel(*refs, is_a):
    if is_a:
        (qt_ref, k_ref, vt_ref, g_ref, tab_ref, lam_ref, ng_ref,
         y_ref, qtb_ref, m_ref, acc_ref, s_ref, mt_ref) = refs
        e_ref = None
    else:
        (qt_ref, k_ref, e_ref, vt_ref, g_ref, tab_ref, kn_ref, nc_ref,
         y_ref, qtb_ref, m_ref, acc_ref, s_ref, mt_ref) = refs
    p = pl.program_id(1)
    qi = pl.program_id(2)

    qt = qt_ref[0].astype(F32)
    row = lax.broadcasted_iota(jnp.int32, (LANES, T), 0)
    q_lo = jnp.where(row < HALF, qt, 0.0)
    q_hi = jnp.where(row >= HALF, qt, 0.0)
    qtb_ref[:LANES, :] = jnp.concatenate([q_lo, q_hi], axis=1).astype(BF16)
    if not is_a:
        r2 = lax.broadcasted_iota(jnp.int32, (LANES, 2 * T), 0)
        c2 = lax.broadcasted_iota(jnp.int32, (LANES, 2 * T), 1)
        head = 2 * p + jnp.where(c2 >= T, 1, 0)
        sel = (r2 == head) | (r2 == head + B_HEADS) | (r2 == head + 2 * B_HEADS)
        qtb_ref[LANES:, :] = jnp.where(sel, 1.0, 0.0).astype(BF16)

    def scores(j):
        rows = pl.ds(pl.multiple_of(j * T, T), T)
        kt = k_ref[0, rows, :]
        if not is_a:
            kt = jnp.concatenate([kt, e_ref[0, rows, :]], axis=1)
        return jnp.dot(kt, qtb_ref[...], preferred_element_type=F32)

    def pv_dot(vts, pb):
        if is_a:
            return jnp.dot(vts[0], pb, preferred_element_type=F32)
        return jnp.concatenate(
            [jnp.dot(vts[0], pb[:, :T], preferred_element_type=F32),
             jnp.dot(vts[1], pb[:, T:], preferred_element_type=F32)], axis=1)

    def softmax_pv(vts, s, mt, penalty=None):
        m_old = m_ref[...]
        if penalty is None:
            m_new = jnp.maximum(m_old, mt)
            shift = m_new
        else:
            m_new = jnp.maximum(m_old, mt + penalty)
            shift = m_new - penalty
        alpha = jnp.exp2(m_old - m_new)
        pv = pv_dot(vts, jnp.exp2(s - shift).astype(BF16))
        acc_ref[...] = alpha * acc_ref[...] + pv
        m_ref[...] = m_new

    def v_tile(j, valid=None):
        jc = jnp.maximum(j, 0)
        vts = [vt_ref[0, u, jc] for u in range(1 if is_a else 2)]
        if valid is not None:
            vts = [jnp.where(valid, vt, jnp.zeros_like(vt)) for vt in vts]
        return vts

    def scores_to(slot, j):
        s = scores(jnp.maximum(j, 0))
        s_ref[slot, :, :2 * T] = s
        mt_ref[slot] = jnp.max(s, axis=0, keepdims=True)

    nblk = T // BLK
    tab_sub = tab_ref[0, 0] if is_a else None
    tab_diag = tab_ref[0, 1]

    def diag_tile(s):
        pbs, mts = [], []
        for lb in range(2 * nblk):
            qb = lb % nblk
            lanes = slice(lb * BLK, (lb + 1) * BLK)
            parts = [s[:(qb - 1) * BLK, lanes]] if qb >= 2 else []
            if qb >= 1:
                below = s[(qb - 1) * BLK:qb * BLK, lanes]
                parts.append(below if tab_sub is None else below + tab_sub)
            parts.append(s[qb * BLK:(qb + 1) * BLK, lanes] + tab_diag)
            sb = parts[0] if len(parts) == 1 else jnp.concatenate(parts, axis=0)
            mt = jnp.max(sb, axis=0, keepdims=True)
            pb = jnp.exp2(sb - mt).astype(BF16)
            if qb < nblk - 1:
                pb = jnp.concatenate([pb, jnp.zeros((T - (qb + 1) * BLK, BLK), BF16)], axis=0)
            pbs.append(pb)
            mts.append(mt)
        acc_ref[...] = pv_dot(v_tile(qi), jnp.concatenate(pbs, axis=1))
        m_ref[...] = jnp.concatenate(mts, axis=1)

    s_diag = scores(qi)
    if is_a:
        s_near = scores(jnp.maximum(qi - 1, 0))
        last = s_near[T - BLK:]
        last = jnp.concatenate(
            [last[:, :BLK] + tab_sub, last[:, BLK:T], last[:, T:T + BLK] + tab_sub,
             last[:, T + BLK:]], axis=1)
        s_near = jnp.concatenate([s_near[:T - BLK], last], axis=0)
        diag_tile(s_diag)
        top = qi - 2
        scores_to(0, top)
        softmax_pv(v_tile(qi - 1), s_near, jnp.max(s_near, axis=0, keepdims=True),
                   penalty=jnp.where(qi >= 1, 0.0, NEG_INF))
        n_far = jnp.maximum(qi - 1, 0)
    else:
        top = qi - 1
        scores_to(0, top)
        diag_tile(s_diag)
        need = None
        for half, qh in enumerate((q_lo, q_hi)):
            qn = jnp.sqrt(jnp.max(jnp.sum(qh * qh, axis=0, keepdims=True), axis=1, keepdims=True))
            m_min = jnp.min(m_ref[:, half * T:(half + 1) * T], axis=1, keepdims=True)
            bound = (SKIP_NORM_MARGIN * qn) * kn_ref[0, half] + SKIP_ABS_MARGIN
            nd = (nc_ref[0, half] + bound) > (m_min - SKIP_THRESH)
            need = nd if need is None else (need | nd)
        tiles = need.shape[0]
        jidx = lax.broadcasted_iota(jnp.int32, need.shape, 0)
        cand = jnp.where(need & (jidx < qi), jidx, qi)
        n_far = qi - jnp.min(cand.reshape(tiles * 8, LANES))

    def far_steps(t0, count):
        for u in range(count):
            t = t0 + u
            scores_to((u + 1) % 2, top - t - 1)
            valid = None if u % 2 == 0 else t < n_far
            softmax_pv(v_tile(top - t, valid), s_ref[u % 2, :, :2 * T], mt_ref[u % 2])

    done = 0
    left = n_far + (n_far & 1)
    for unroll in FAR_UNROLLS:
        trips = left // unroll

        def body(i, carry, unroll=unroll, done=done):
            far_steps(done + unroll * i, unroll)
            return carry

        lax.fori_loop(0, trips, body, 0)
        done = done + unroll * trips
        left = left - unroll * trips

    vd = acc_ref.shape[0] - ONES_ROWS
    o = acc_ref[:vd] * (1.0 / acc_ref[vd:vd + 1])
    if is_a:
        lam = lam_ref[0:1, 0:1]
        o2 = o[:, :T] - lam * o[:, T:]
        ms = jnp.mean(o2 * o2, axis=0, keepdims=True)
        o2 = o2 * lax.rsqrt(ms + LN_EPS) * ng_ref[...] * (1.0 - LAM_INIT)
    else:
        o2 = jnp.concatenate([o[:, :T], o[:, T:]], axis=0)
    y_ref[0] = (o2 * g_ref[0].astype(F32)).astype(BF16)


def _attn_call(qt, k, e, vt5, g, tab, lam, ng, kn, nc, *, is_a):
    b, d, s = qt.shape
    nq = s // T
    off = 0 if is_a else PAIRS
    kd = LANES if is_a else 2 * LANES
    v_rows = vt5.shape[3]
    in_specs = [
        pl.BlockSpec((1, LANES, T), lambda i, p, q: (i, p + off, q)),
        pl.BlockSpec((1, s, LANES), lambda i, p, q: (i, 0, p + off)),
    ]
    args = [qt, k]
    if not is_a:
        in_specs.append(pl.BlockSpec((1, s, LANES), lambda i, p, q: (i, 0, 0)))
        args.append(e)
    in_specs += [
        pl.BlockSpec((1, 1 if is_a else 2, s // TM, v_rows, TM), lambda i, p, q: (i, p, 0, 0, 0)),
        pl.BlockSpec((1, LANES, T), lambda i, p, q: (i, p + off, q)),
    ]
    args += [vt5, g]
    if is_a:
        in_specs.append(pl.BlockSpec((1, 2, BLK, BLK), lambda i, p, q: (p, 0, 0, 0)))
        in_specs.append(pl.BlockSpec((8, LANES), lambda i, p, q: (0, 0)))
        in_specs.append(pl.BlockSpec((LANES, 1), lambda i, p, q: (0, 0)))
        args += [tab, lam, ng]
    else:
        in_specs.append(pl.BlockSpec((1, 2, BLK, BLK), lambda i, p, q: (A_HEADS, 0, 0, 0)))
        stat_spec = pl.BlockSpec((1, 2, s // TM, 8, LANES), lambda i, p, q: (i, p, 0, 0, 0))
        in_specs += [stat_spec, stat_spec]
        args += [tab, kn, nc]
    return pl.pallas_call(
        functools.partial(_attn_kernel, is_a=is_a),
        grid=(b, PAIRS, nq),
        in_specs=in_specs,
        out_specs=pl.BlockSpec((1, LANES, T), lambda i, p, q: (i, p, q)),
        out_shape=jax.ShapeDtypeStruct((b, PAIRS * LANES, s), BF16),
        scratch_shapes=[
            pltpu.VMEM((kd, 2 * T), BF16),
            pltpu.VMEM((1, 2 * T), F32),
            pltpu.VMEM((v_rows, 2 * T), F32),
            pltpu.VMEM((2, T, 2 * T + LANES), F32),
            pltpu.VMEM((2, 1, 2 * T), F32),
        ],
        compiler_params=pltpu.CompilerParams(
            dimension_semantics=("arbitrary", "arbitrary", "arbitrary"),
            vmem_limit_bytes=VMEM_LIMIT),
        name="attn_a" if is_a else "attn_b",
    )(*args)


def _out_kernel(ya_ref, yb_ref, w1_ref, w2_ref, x_ref, gate_ref, lng_ref, lnb_ref, o_ref):
    tn = (((0,), (0,)), ((), ()))
    y = (lax.dot_general(ya_ref[0], w1_ref[...], tn, preferred_element_type=F32)
         + lax.dot_general(yb_ref[0], w2_ref[...], tn, preferred_element_type=F32))
    z = DEEPNORM_ALPHA * x_ref[0] + gate_ref[0] * y
    mu = jnp.mean(z, axis=-1, keepdims=True)
    zc = z - mu
    var = jnp.mean(zc * zc, axis=-1, keepdims=True)
    o_ref[0] = zc * lax.rsqrt(var + LN_EPS) * lng_ref[...] + lnb_ref[...]


def _out_call(ya, yb, w1, w2, x, gate, lng, lnb):
    b, s, d = x.shape
    half = ya.shape[1]
    return pl.pallas_call(
        _out_kernel,
        grid=(b, s // TO),
        in_specs=[
            pl.BlockSpec((1, half, TO), lambda i, t: (i, 0, t)),
            pl.BlockSpec((1, half, TO), lambda i, t: (i, 0, t)),
            pl.BlockSpec((half, d), lambda i, t: (0, 0)),
            pl.BlockSpec((half, d), lambda i, t: (0, 0)),
            pl.BlockSpec((1, TO, d), lambda i, t: (i, t, 0)),
            pl.BlockSpec((1, 1, d), lambda i, t: (i, 0, 0)),
            pl.BlockSpec((1, d), lambda i, t: (0, 0)),
            pl.BlockSpec((1, d), lambda i, t: (0, 0)),
        ],
        out_specs=pl.BlockSpec((1, TO, d), lambda i, t: (i, t, 0)),
        out_shape=jax.ShapeDtypeStruct((b, s, d), x.dtype),
        compiler_params=pltpu.CompilerParams(
            dimension_semantics=("arbitrary", "arbitrary"), vmem_limit_bytes=VMEM_LIMIT),
        name="out",
    )(ya, yb, w1, w2, x, gate, lng, lnb)


def kernel(x, c, w_in, w_out, rel_bias, lam_q1, lam_k1, lam_q2, lam_k2, diff_norm_g,
           b_forget, w_ada, b_ada, ln_g, ln_b):
    b, s, d = x.shape
    assert s % T == 0 and s % TO == 0 and T == TM and d == PAIRS * 2 * LANES
    layer = 0
    aw = PAIRS * LANES

    c8 = jnp.pad(c, ((0, 8 - b), (0, 0)))
    ada8, lam = _ada_call(c8, w_ada[layer], b_ada[layer][None], lam_q1[layer][None],
                          lam_k1[layer][None], lam_q2[layer][None], lam_k2[layer][None])
    ada = ada8[:b]
    shift = ada[:, None, :d]
    scale = ada[:, None, d:2 * d]
    gate = ada[:, None, 2 * d:]

    w = w_in[layer]
    grp = lambda i: w[:, i * aw:(i + 1) * aw]
    order = jnp.argsort(b_forget[layer])
    by_head = lambda m: m.reshape(d, B_HEADS, HALF)[:, order].reshape(d, aw)
    wbf = jnp.pad(w[:, 8 * aw:][:, order], ((0, 0), (0, B_HEADS)))
    wrow = jnp.concatenate([grp(1), by_head(grp(5))], axis=1).astype(BF16)
    wt = jnp.concatenate([grp(0), by_head(grp(4)), grp(2), by_head(grp(6)), grp(3),
                          by_head(grp(7)), wbf], axis=1).T.astype(BF16)
    bfb = b_forget[layer][order].reshape(B_HEADS, 1).astype(F32)

    k, g, qt, vta, vtb, e, kn, nc = _proj_call(x, shift, scale, wrow, wt, bfb)
    tab = _table_call(rel_bias)
    ng = diff_norm_g[layer].reshape(LANES, 1)
    ya = _attn_call(qt, k, None, vta, g, tab, lam, ng, None, None, is_a=True)
    yb = _attn_call(qt, k, e, vtb, g, tab, None, None, kn, nc, is_a=False)

    wo = w_out[layer].astype(BF16)
    wo_b = wo[aw:].reshape(B_HEADS, HALF, d)[order].reshape(aw, d)
    return _out_call(ya, yb, wo[:aw], wo_b, x, gate, ln_g[layer][None], ln_b[layer][None])
```
